```python
import jax, jax.numpy as jnp
from jax import lax
import numpy as np

D_MODEL = 2048
BATCH = 2
SEQ = 16384
DEPTH = 2

HEAD_DIM = 128
D_FF = 5632
PLE_DIM = 256
NORM_EPS = 1e-6
NEG_INF = -1e30
POOL_WINDOWS = (2, 4, 8, 16)
POOL_GROUPS = 4
POOL_WIDTH = 1024
POOL_GROUP_WIDTH = POOL_WIDTH // POOL_GROUPS
ATTN_PATTERNS = ((128, 1), (512, 4), (2048, 16))
ATTN_HEADS_PER_GROUP = 4
ATTN_HEADS = ATTN_HEADS_PER_GROUP * len(ATTN_PATTERNS)
ATTN_WIDTH = ATTN_HEADS * HEAD_DIM
REL_BUCKETS = 32
REL_MAX_DISTANCE = 1024
RET_HEADS = 4
RET_QK_DIM = 128
RET_V_DIM = 256
RET_QK_WIDTH = RET_HEADS * RET_QK_DIM
RET_V_WIDTH = RET_HEADS * RET_V_DIM
RET_CHUNK = 128
ROPE_BASE = 10000.0
CONV_WIDTH = 1024
CONV_SIZE = 31
N_BRANCHES = 4
IN_SPLITS = (POOL_WIDTH, ATTN_WIDTH, ATTN_WIDTH, ATTN_WIDTH,
             RET_QK_WIDTH, RET_QK_WIDTH, RET_V_WIDTH, RET_V_WIDTH, 2 * CONV_WIDTH)
IN_WIDTH = POOL_WIDTH + 3 * ATTN_WIDTH + 2 * RET_QK_WIDTH + 2 * RET_V_WIDTH + 2 * CONV_WIDTH

kernel_name = "hybrid_parallel_encoder_block"

F32 = jnp.float32


def _rms_norm(x, gain):
    xf = x.astype(F32)
    xf = xf * lax.rsqrt(jnp.mean(xf * xf, axis=-1, keepdims=True) + NORM_EPS)
    return (xf * gain.astype(F32)).astype(x.dtype)


def _layer_norm(x, gain, bias):
    xf = x.astype(F32)
    mu = jnp.mean(xf, axis=-1, keepdims=True)
    var = jnp.mean(jnp.square(xf - mu), axis=-1, keepdims=True)
    return ((xf - mu) * lax.rsqrt(var + NORM_EPS) * gain.astype(F32) + bias.astype(F32)).astype(x.dtype)


def _swiglu(x, w_gate, w_up, w_down):
    return (jax.nn.silu(x @ w_gate) * (x @ w_up)) @ w_down


def _pool_mixer(xp, pool_w, pool_scale):
    B, S, _ = xp.shape
    xg = xp.reshape(B, S, POOL_GROUPS, POOL_GROUP_WIDTH).astype(F32)
    cs = jnp.concatenate([jnp.zeros_like(xg[:, :1]), jnp.cumsum(xg, axis=1)], axis=1)
    t = np.arange(S)[:, None]
    w = np.array(POOL_WINDOWS)[None, :]
    lo = np.clip(t - w // 2, 0, S)
    hi = np.clip(t - w // 2 + w, 0, S)
    cnt = (hi - lo).astype(np.float32)
    g = np.arange(POOL_GROUPS)[None, :]
    mean = (cs[:, hi, g] - cs[:, lo, g]) / cnt[None, :, :, None]
    mixed = (mean - xg).astype(xp.dtype)
    y = jnp.einsum('bsgc,gcd->bsgd', mixed, pool_w).reshape(B, S, POOL_WIDTH)
    return y * pool_scale


def _t5_bucket(rel):
    half = REL_BUCKETS // 2
    exact = half // 2
    offset = np.where(rel > 0, half, 0)
    n = np.abs(rel)
    large = exact + (np.log(np.maximum(n, 1) / exact) / np.log(REL_MAX_DISTANCE / exact)
                     * (half - exact)).astype(np.int32)
    large = np.minimum(large, half - 1)
    return (offset + np.where(n < exact, n, large)).astype(np.int32)


def _dilated_group(q, k, v, bias_table, window, dilation):
    B, S, Hg, Dh = q.shape
    radius = window // (2 * dilation)
    blk = radius
    L = S // dilation
    nblk = -(-L // blk)
    Lp = nblk * blk

    def to_sub(t):
        return t.reshape(B, L, dilation, Hg, Dh).transpose(0, 2, 1, 3, 4)

    def from_sub(t):
        t = t.reshape((B, dilation, Lp) + t.shape[4:])[:, :, :L]
        return jnp.swapaxes(t, 1, 2).reshape((B, S) + t.shape[3:])

    qs, ks, vs = to_sub(q), to_sub(k), to_sub(v)
    qb = jnp.pad(qs, ((0, 0), (0, 0), (0, Lp - L), (0, 0), (0, 0))).reshape(B, dilation, nblk, blk, Hg, Dh)
    pad = ((0, 0), (0, 0), (blk, Lp - L + blk), (0, 0), (0, 0))

    def band_windows(t):
        tb = jnp.pad(t, pad).reshape(B, dilation, nblk + 2, blk, Hg, Dh)
        return jnp.concatenate([tb[:, :, :-2], tb[:, :, 1:-1], tb[:, :, 2:]], axis=3)

    kw, vw = band_windows(ks), band_windows(vs)
    a = np.arange(blk)[:, None]
    b = np.arange(3 * blk)[None, :]
    rel_sub = b - blk - a
    band = np.abs(rel_sub) <= radius
    kpos = np.arange(nblk)[:, None] * blk - blk + np.arange(3 * blk)[None, :]
    valid = (kpos >= 0) & (kpos < L)
    mask = band[None] & valid[:, None, :]
    bias = jnp.take(bias_table, _t5_bucket(rel_sub * dilation), axis=0)
    bias = bias.transpose(2, 0, 1).astype(F32)

    logits = jnp.einsum('bdnqhc,bdnkhc->bdnhqk', qb, kw) + bias[None, None, None]
    logits = jnp.where(mask[None, None, :, None], logits, NEG_INF)
    m = jnp.max(logits, axis=-1, keepdims=True)
    e = jnp.exp(logits - m)
    s = jnp.sum(e, axis=-1, keepdims=True)
    o = jnp.einsum('bdnhqk,bdnkhc->bdnqhc', e, vw) / s.transpose(0, 1, 2, 4, 3, 5)
    lse = (m + jnp.log(s))[..., 0].transpose(0, 1, 2, 4, 3)
    return from_sub(o), from_sub(lse)


def _dilated_attention(q, k, v, q_norm, k_norm, rel_bias):
    B, S = q.shape[:2]
    dtype = q.dtype
    qn = _rms_norm(q, q_norm).astype(F32) * (HEAD_DIM ** -0.5)
    kn = _rms_norm(k, k_norm).astype(F32)
    vf = v.astype(F32)
    outs, lses = [], []
    for g, (window, dilation) in enumerate(ATTN_PATTERNS):
        hs = slice(g * ATTN_HEADS_PER_GROUP, (g + 1) * ATTN_HEADS_PER_GROUP)
        o, lse = _dilated_group(qn[:, :, hs], kn[:, :, hs], vf[:, :, hs], rel_bias[:, hs], window, dilation)
        outs.append(o)
        lses.append(lse)
    alpha = jax.nn.softmax(jnp.stack(lses, axis=0), axis=0)
    y = jnp.concatenate([outs[g] * alpha[g][..., None] for g in range(len(ATTN_PATTERNS))], axis=2)
    return y.reshape(B, S, ATTN_WIDTH).astype(dtype)


def _rotary(x, pos):
    half = x.shape[-1] // 2
    inv = ROPE_BASE ** (-jnp.linspace(0.0, 1.0, half, dtype=F32))
    ang = pos[:, None] * inv[None, :]
    cos = jnp.cos(ang)[None, :, None, :]
    sin = jnp.sin(ang)[None, :, None, :]
    x1, x2 = x[..., :half], x[..., half:]
    return jnp.concatenate([x1 * cos - x2 * sin, x1 * sin + x2 * cos], axis=-1)


def _retention_one_direction(q, k, v, log_gamma, include_diag):
    B, S, H, dk = q.shape
    dv = v.shape[-1]
    C = RET_CHUNK
    N = S // C

    def chunks(t):
        return t.reshape(B, N, C, H, t.shape[-1]).transpose(1, 0, 3, 2, 4)

    idx = np.arange(C, dtype=np.float32)
    diff = idx[:, None] - idx[None, :]
    tri = (diff >= 0) if include_diag else (diff > 0)
    decay_intra = jnp.where(tri[None], jnp.exp(np.where(tri, diff, 0.0)[None] * log_gamma[:, None, None]), 0.0)
    xi = jnp.exp((idx + 1.0)[None, :] * log_gamma[:, None])
    zeta = jnp.exp((C - 1.0 - idx)[None, :] * log_gamma[:, None])
    g_chunk = jnp.exp(C * log_gamma)

    def step(state, qkv):
        qc, kc, vc = qkv
        scores = jnp.einsum('bhqc,bhkc->bhqk', qc, kc) * decay_intra
        inner = jnp.einsum('bhqk,bhkv->bhqv', scores, vc)
        cross = jnp.einsum('bhqc,bhcv->bhqv', qc * xi[..., None], state)
        state = state * g_chunk[:, None, None] + jnp.einsum('bhkc,bhkv->bhcv', kc * zeta[..., None], vc)
        return state, inner + cross

    state0 = jnp.zeros((B, H, dk, dv), F32)
    _, out = lax.scan(step, state0, (chunks(q), chunks(k), chunks(v)))
    return out.transpose(1, 0, 3, 2, 4).reshape(B, S, H, dv)


def _retention(q, k, v, gate, decay_logit, ret_norm):
    B, S, _ = q.shape
    dtype = q.dtype
    pos = jnp.arange(S, dtype=F32)
    qr = _rotary(q.reshape(B, S, RET_HEADS, RET_QK_DIM).astype(F32), pos)
    kr = _rotary(k.reshape(B, S, RET_HEADS, RET_QK_DIM).astype(F32), pos) * (RET_QK_DIM ** -0.5)
    vr = v.reshape(B, S, RET_HEADS, RET_V_DIM).astype(F32)
    log_gamma = jax.nn.log_sigmoid(decay_logit.astype(F32))
    fwd = _retention_one_direction(qr, kr, vr, log_gamma[0], True)
    bwd = _retention_one_direction(qr[:, ::-1], kr[:, ::-1], vr[:, ::-1], log_gamma[1], False)[:, ::-1]
    y = fwd + bwd
    y = y * lax.rsqrt(jnp.mean(y * y, axis=-1, keepdims=True) + NORM_EPS)
    y = y * ret_norm.reshape(RET_HEADS, RET_V_DIM).astype(F32)
    return y.reshape(B, S, RET_V_WIDTH).astype(dtype) * jax.nn.silu(gate)


def _conv_module(c_in, conv_w, conv_b, norm_g, norm_b):
    a, g = jnp.split(c_in, 2, axis=-1)
    h = a * jax.nn.sigmoid(g)
    h = lax.conv_general_dilated(h, conv_w[:, None, :].astype(h.dtype), window_strides=(1,),
                                 padding=[(CONV_SIZE // 2, CONV_SIZE // 2)],
                                 dimension_numbers=('NWC', 'WIO', 'NWC'),
                                 feature_group_count=CONV_WIDTH) + conv_b
    return jax.nn.silu(_layer_norm(h, norm_g, norm_b))


def setup_inputs(seed: int = 0) -> dict:
    key = jax.random.key(seed)
    ks = iter(jax.random.split(key, 48))
    L = DEPTH

    def w(shape, fan_in):
        return jax.random.normal(next(ks), shape, F32) * (fan_in ** -0.5)

    def gain(shape):
        return 1.0 + 0.05 * jax.random.normal(next(ks), shape, F32)

    def small(shape, s=0.02):
        return s * jax.random.normal(next(ks), shape, F32)

    decay_base = jnp.asarray(np.log(2.0 ** (5 + np.arange(RET_HEADS)) - 1.0).astype(np.float32))
    return {
        "x": jax.random.normal(next(ks), (BATCH, SEQ, D_MODEL), F32),
        "p": jax.random.normal(next(ks), (DEPTH, BATCH, SEQ, PLE_DIM), F32),
        "rel_bias": small((REL_BUCKETS, ATTN_HEADS), 0.3),
        "ffn1_norm": gain((L, D_MODEL)),
        "ffn1_w_gate": w((L, D_MODEL, D_FF), D_MODEL),
        "ffn1_w_up": w((L, D_MODEL, D_FF), D_MODEL),
        "ffn1_w_down": w((L, D_FF, D_MODEL), D_FF),
        "mix_norm": gain((L, D_MODEL)),
        "w_in": w((L, D_MODEL, IN_WIDTH), D_MODEL),
        "pool_w": w((L, POOL_GROUPS, POOL_GROUP_WIDTH, POOL_GROUP_WIDTH), POOL_GROUP_WIDTH),
        "pool_scale": gain((L, POOL_WIDTH)),
        "q_norm": gain((L, HEAD_DIM)),
        "k_norm": gain((L, HEAD_DIM)),
        "ret_decay_logit": decay_base[None, None, :] + small((L, 2, RET_HEADS), 0.1),
        "ret_norm": gain((L, RET_V_WIDTH)),
        "conv_w": w((L, CONV_SIZE, CONV_WIDTH), CONV_SIZE),
        "conv_b": small((L, CONV_WIDTH)),
        "conv_norm_g": gain((L, CONV_WIDTH)),
        "conv_norm_b": small((L, CONV_WIDTH)),
        "w_gate": w((L, D_MODEL, N_BRANCHES * D_MODEL), D_MODEL),
        "b_gate": small((L, N_BRANCHES * D_MODEL)),
        "w_br_pool": w((L, POOL_WIDTH, D_MODEL), POOL_WIDTH),
        "w_br_attn": w((L, ATTN_WIDTH, D_MODEL), ATTN_WIDTH),
        "w_br_ret": w((L, RET_V_WIDTH, D_MODEL), RET_V_WIDTH),
        "w_br_conv": w((L, CONV_WIDTH, D_MODEL), CONV_WIDTH),
        "w_out": w((L, D_MODEL, D_MODEL), D_MODEL),
        "ffn2_norm": gain((L, D_MODEL)),
        "ffn2_w_gate": w((L, D_MODEL, D_FF), D_MODEL),
        "ffn2_w_up": w((L, D_MODEL, D_FF), D_MODEL),
        "ffn2_w_down": w((L, D_FF, D_MODEL), D_FF),
        "ple_norm": gain((L, D_MODEL)),
        "w_ple_gate": w((L, D_MODEL, D_MODEL), D_MODEL),
        "w_ple_proj": w((L, PLE_DIM, D_MODEL), PLE_DIM),
    }


def reference(x, p, rel_bias, ffn1_norm, ffn1_w_gate, ffn1_w_up, ffn1_w_down, mix_norm, w_in,
              pool_w, pool_scale, q_norm, k_norm, ret_decay_logit, ret_norm, conv_w, conv_b,
              conv_norm_g, conv_norm_b, w_gate, b_gate, w_br_pool, w_br_attn, w_br_ret, w_br_conv,
              w_out, ffn2_norm, ffn2_w_gate, ffn2_w_up, ffn2_w_down, ple_norm, w_ple_gate, w_ple_proj):
    B, S, _ = x.shape
    split_at = np.cumsum(IN_SPLITS)[:-1]
    h = x
    for i in range(DEPTH):
        h = h + 0.5 * _swiglu(_rms_norm(h, ffn1_norm[i]), ffn1_w_gate[i], ffn1_w_up[i], ffn1_w_down[i])
        u = _rms_norm(h, mix_norm[i])
        xp, aq, ak, av, rq, rk, rv, rg, cin = jnp.split(u @ w_in[i], split_at, axis=-1)
        y_pool = _pool_mixer(xp, pool_w[i], pool_scale[i])
        y_attn = _dilated_attention(aq.reshape(B, S, ATTN_HEADS, HEAD_DIM),
                                    ak.reshape(B, S, ATTN_HEADS, HEAD_DIM),
                                    av.reshape(B, S, ATTN_HEADS, HEAD_DIM),
                                    q_norm[i], k_norm[i], rel_bias)
        y_ret = _retention(rq, rk, rv, rg, ret_decay_logit[i], ret_norm[i])
        y_conv = _conv_module(cin, conv_w[i], conv_b[i], conv_norm_g[i], conv_norm_b[i])
        gates = jax.nn.sigmoid(u @ w_gate[i] + b_gate[i]).reshape(B, S, N_BRANCHES, D_MODEL)
        merged = (gates[:, :, 0] * (y_pool @ w_br_pool[i])
                  + gates[:, :, 1] * (y_attn @ w_br_attn[i])
                  + gates[:, :, 2] * (y_ret @ w_br_ret[i])
                  + gates[:, :, 3] * (y_conv @ w_br_conv[i]))
        h = h + merged @ w_out[i]
        h = h + 0.5 * _swiglu(_rms_norm(h, ffn2_norm[i]), ffn2_w_gate[i], ffn2_w_up[i], ffn2_w_down[i])
        ple = p[i] @ w_ple_proj[i]
        h = h + jax.nn.sigmoid(_rms_norm(h, ple_norm[i]) @ w_ple_gate[i]) * ple
    return h
```

```python
import functools

import numpy as np
import jax
import jax.numpy as jnp
from jax import lax
from jax.experimental import pallas as pl
from jax.experimental.pallas import tpu as pltpu

F32 = jnp.float32
BF16 = jnp.bfloat16

D_MODEL = 2048
HEAD_DIM = 128
NORM_EPS = 1e-6
NEG_INF = -1e30
POOL_WINDOWS = (2, 4, 8, 16)
POOL_WIDTH = 1024
POOL_GROUP_WIDTH = 256
POOL_HALO = 8
ATTN_PATTERNS = ((128, 1), (512, 4), (2048, 16))
ATTN_HEADS_PER_GROUP = 4
ATTN_HEADS = 12
ATTN_WIDTH = ATTN_HEADS * HEAD_DIM
ATTN_GROUP_WIDTH = ATTN_HEADS_PER_GROUP * HEAD_DIM
ATTN_RADIUS = 64
REL_BUCKETS = 32
REL_MAX_DISTANCE = 1024
RET_HEADS = 4
RET_QK_DIM = 128
RET_V_DIM = 256
RET_QK_WIDTH = RET_HEADS * RET_QK_DIM
RET_V_WIDTH = RET_HEADS * RET_V_DIM
ROPE_BASE = 10000.0
CONV_WIDTH = 1024
CONV_SIZE = 31
CONV_HALO = 16
N_BRANCHES = 4

IN_WIDTH = 10752
COL_CONV_A = 0
COL_CONV_G = 1024
COL_POOL = 2048
COL_RET_V = 3072
COL_RET_G = 4096
COL_ATT_Q = 5120
COL_ATT_K = 6656
COL_ATT_V = 8192
COL_RET_Q = 9728
COL_RET_K = 10240

VMEM_LIMIT = 56 * 1024 * 1024


def _params(*sem):
    return pltpu.CompilerParams(dimension_semantics=sem, vmem_limit_bytes=VMEM_LIMIT)


def _rms(x, gain):
    return x * lax.rsqrt(jnp.mean(x * x, axis=-1, keepdims=True) + NORM_EPS) * gain


def _ffn_body(h_ref, g_ref, wg_ref, wu_ref, wd_ref, o_ref, xn_ref, acc_ref):
    j = pl.program_id(1)

    @pl.when(j == 0)
    def _():
        xn_ref[...] = _rms(h_ref[...], g_ref[...]).astype(BF16)
        acc_ref[...] = jnp.zeros_like(acc_ref)

    xn = xn_ref[...]
    a = jnp.dot(xn, wg_ref[...], preferred_element_type=F32)
    b = jnp.dot(xn, wu_ref[...], preferred_element_type=F32)
    mid = (a * jax.nn.sigmoid(a) * b).astype(BF16)
    acc_ref[...] += jnp.dot(mid, wd_ref[...], preferred_element_type=F32)

    @pl.when(j == pl.num_programs(1) - 1)
    def _():
        o_ref[...] = h_ref[...] + 0.5 * acc_ref[...]


def _ffn(h, gain, wg, wu, wd, *, tm, tf):
    T, D = h.shape
    Fd = wg.shape[1]
    return pl.pallas_call(
        _ffn_body,
        grid=(T // tm, Fd // tf),
        in_specs=[
            pl.BlockSpec((tm, D), lambda i, j: (i, 0)),
            pl.BlockSpec((1, D), lambda i, j: (0, 0)),
            pl.BlockSpec((D, tf), lambda i, j: (0, j)),
            pl.BlockSpec((D, tf), lambda i, j: (0, j)),
            pl.BlockSpec((tf, D), lambda i, j: (j, 0)),
        ],
        out_specs=pl.BlockSpec((tm, D), lambda i, j: (i, 0)),
        out_shape=jax.ShapeDtypeStruct((T, D), F32),
        scratch_shapes=[pltpu.VMEM((tm, D), BF16), pltpu.VMEM((tm, D), F32)],
        compiler_params=_params("parallel", "arbitrary"),
        name="ffn",
    )(h, gain, wg, wu, wd)


def _inproj_body(h_ref, g_ref, w_ref, o_ref, xn_ref):
    @pl.when(pl.program_id(1) == 0)
    def _():
        xn_ref[...] = _rms(h_ref[...], g_ref[...]).astype(BF16)

    o_ref[...] = jnp.dot(xn_ref[...], w_ref[...], preferred_element_type=F32)


def _inproj(h, gain, w, *, tm, tn):
    T, D = h.shape
    N = w.shape[1]
    return pl.pallas_call(
        _inproj_body,
        grid=(T // tm, N // tn),
        in_specs=[
            pl.BlockSpec((tm, D), lambda i, j: (i, 0)),
            pl.BlockSpec((1, D), lambda i, j: (0, 0)),
            pl.BlockSpec((D, tn), lambda i, j: (0, j)),
        ],
        out_specs=pl.BlockSpec((tm, tn), lambda i, j: (i, j)),
        out_shape=jax.ShapeDtypeStruct((T, N), F32),
        scratch_shapes=[pltpu.VMEM((tm, D), BF16)],
        compiler_params=_params("parallel", "arbitrary"),
        name="inproj",
    )(h, gain, w)


def _pool_body(cur_ref, prev_ref, next_ref, w_ref, sc_ref, o_ref, ext_ref, *, tm, S):
    pos0 = (pl.program_id(0) * tm) % S
    ext_ref[0:POOL_HALO, :] = jnp.where(pos0 == 0, 0.0, prev_ref[...])
    ext_ref[POOL_HALO:POOL_HALO + tm, :] = cur_ref[...]
    ext_ref[POOL_HALO + tm:, :] = jnp.where(pos0 + tm == S, 0.0, next_ref[...])
    t = pos0 + lax.broadcasted_iota(jnp.int32, (tm, 1), 0)
    for g, w in enumerate(POOL_WINDOWS):
        cs = slice(g * POOL_GROUP_WIDTH, (g + 1) * POOL_GROUP_WIDTH)
        tot = ext_ref[POOL_HALO - w // 2:POOL_HALO - w // 2 + tm, cs]
        for k in range(-w // 2 + 1, w // 2):
            tot = tot + ext_ref[POOL_HALO + k:POOL_HALO + k + tm, cs]
        cnt = jnp.minimum(t + w // 2, S) - jnp.maximum(t - w // 2, 0)
        mixed = tot / cnt.astype(F32) - cur_ref[:, cs]
        y = jnp.dot(mixed.astype(BF16), w_ref[g], preferred_element_type=F32)
        o_ref[:, cs] = (y * sc_ref[:, cs]).astype(o_ref.dtype)


def _pool(proj, pool_w, pool_scale, *, S, tm):
    T = proj.shape[0]
    cb = COL_POOL // POOL_WIDTH
    hb = tm // POOL_HALO
    last = T // POOL_HALO - 1
    return pl.pallas_call(
        functools.partial(_pool_body, tm=tm, S=S),
        grid=(T // tm,),
        in_specs=[
            pl.BlockSpec((tm, POOL_WIDTH), lambda i: (i, cb)),
            pl.BlockSpec((POOL_HALO, POOL_WIDTH), lambda i: (jnp.maximum(i * hb - 1, 0), cb)),
            pl.BlockSpec((POOL_HALO, POOL_WIDTH), lambda i: (jnp.minimum((i + 1) * hb, last), cb)),
            pl.BlockSpec(pool_w.shape, lambda i: (0, 0, 0)),
            pl.BlockSpec((1, POOL_WIDTH), lambda i: (0, 0)),
        ],
        out_specs=pl.BlockSpec((tm, POOL_WIDTH), lambda i: (i, 0)),
        out_shape=jax.ShapeDtypeStruct((T, POOL_WIDTH), BF16),
        scratch_shapes=[pltpu.VMEM((tm + 2 * POOL_HALO, POOL_WIDTH), F32)],
        compiler_params=_params("parallel"),
        name="pool",
    )(proj, proj, proj, pool_w, pool_scale)


CONV_ROWS = 32


def _conv_body(a_ref, ap_ref, an_ref, g_ref, gp_ref, gn_ref, w_ref, b_ref, ng_ref, nb_ref,
               o_ref, ext_ref, co_ref, *, tm, S):
    pos0 = (pl.program_id(0) * tm) % S
    ext_ref[0:CONV_HALO, :] = jnp.where(pos0 == 0, 0.0, ap_ref[...] * jax.nn.sigmoid(gp_ref[...]))
    ext_ref[CONV_HALO:CONV_HALO + tm, :] = a_ref[...] * jax.nn.sigmoid(g_ref[...])
    ext_ref[CONV_HALO + tm:, :] = jnp.where(pos0 + tm == S, 0.0,
                                            an_ref[...] * jax.nn.sigmoid(gn_ref[...]))
    base = CONV_HALO - CONV_SIZE // 2
    for c in range(tm // CONV_ROWS):
        r0 = c * CONV_ROWS
        acc = ext_ref[r0 + base:r0 + base + CONV_ROWS, :] * w_ref[0:1, :]
        for k in range(1, CONV_SIZE):
            acc = acc + ext_ref[r0 + base + k:r0 + base + k + CONV_ROWS, :] * w_ref[k:k + 1, :]
        co_ref[r0:r0 + CONV_ROWS, :] = acc + b_ref[...]
    hc = co_ref[...]
    mu = jnp.mean(hc, axis=-1, keepdims=True)
    var = jnp.mean(jnp.square(hc - mu), axis=-1, keepdims=True)
    ln = (hc - mu) * lax.rsqrt(var + NORM_EPS) * ng_ref[...] + nb_ref[...]
    o_ref[...] = (ln * jax.nn.sigmoid(ln)).astype(o_ref.dtype)


def _conv(proj, conv_w, conv_b, norm_g, norm_b, *, S, tm):
    T = proj.shape[0]
    ca, cg = COL_CONV_A // CONV_WIDTH, COL_CONV_G // CONV_WIDTH
    hb = tm // CONV_HALO
    last = T // CONV_HALO - 1

    def cur(c):
        return pl.BlockSpec((tm, CONV_WIDTH), lambda i: (i, c))

    def prev(c):
        return pl.BlockSpec((CONV_HALO, CONV_WIDTH), lambda i: (jnp.maximum(i * hb - 1, 0), c))

    def nxt(c):
        return pl.BlockSpec((CONV_HALO, CONV_WIDTH), lambda i: (jnp.minimum((i + 1) * hb, last), c))

    row = pl.BlockSpec((1, CONV_WIDTH), lambda i: (0, 0))
    return pl.pallas_call(
        functools.partial(_conv_body, tm=tm, S=S),
        grid=(T // tm,),
        in_specs=[cur(ca), prev(ca), nxt(ca), cur(cg), prev(cg), nxt(cg),
                  pl.BlockSpec((CONV_SIZE, CONV_WIDTH), lambda i: (0, 0)), row, row, row],
        out_specs=pl.BlockSpec((tm, CONV_WIDTH), lambda i: (i, 0)),
        out_shape=jax.ShapeDtypeStruct((T, CONV_WIDTH), BF16),
        scratch_shapes=[pltpu.VMEM((tm + 2 * CONV_HALO, CONV_WIDTH), F32),
                        pltpu.VMEM((tm, CONV_WIDTH), F32)],
        compiler_params=_params("parallel"),
        name="conv",
    )(proj, proj, proj, proj, proj, proj, conv_w, conv_b, norm_g, norm_b)


def _rotate(x, cos2, sin2):
    return x * cos2 + pltpu.roll(x, RET_QK_DIM // 2, 1) * sin2


def _ret_body(*refs, C, cpb, reverse):
    if reverse:
        (q_ref, k_ref, v_ref, cos_ref, sin_ref, dec_ref, xi_ref, zeta_ref, gch_ref,
         fwd_ref, gate_ref, norm_ref, o_ref, state_ref) = refs
    else:
        (q_ref, k_ref, v_ref, cos_ref, sin_ref, dec_ref, xi_ref, zeta_ref, gch_ref,
         o_ref, state_ref) = refs

    @pl.when(pl.program_id(1) == 0)
    def _():
        state_ref[...] = jnp.zeros_like(state_ref)

    order = range(cpb - 1, -1, -1) if reverse else range(cpb)
    for c in order:
        rows = slice(c * C, (c + 1) * C)
        cos2, sin2 = cos_ref[rows, :], sin_ref[rows, :]
        for h in range(RET_HEADS):
            qk = slice(h * RET_QK_DIM, (h + 1) * RET_QK_DIM)
            vs = slice(h * RET_V_DIM, (h + 1) * RET_V_DIM)
            qr = _rotate(q_ref[rows, qk], cos2, sin2)
            kr = _rotate(k_ref[rows, qk], cos2, sin2) * (RET_QK_DIM ** -0.5)
            vb = v_ref[rows, vs].astype(BF16)
            scores = lax.dot_general(qr.astype(BF16), kr.astype(BF16), (((1,), (1,)), ((), ())),
                                     preferred_element_type=F32) * dec_ref[h]
            inner = jnp.dot(scores.astype(BF16), vb, preferred_element_type=F32)
            state = state_ref[h]
            cross = jnp.dot((qr * xi_ref[h]).astype(BF16), state.astype(BF16),
                            preferred_element_type=F32)
            kv = lax.dot_general((kr * zeta_ref[h]).astype(BF16), vb, (((0,), (0,)), ((), ())),
                                 preferred_element_type=F32)
            state_ref[h] = state * gch_ref[h] + kv
            y = inner + cross
            if reverse:
                y = y + fwd_ref[rows, vs]
                y = y * lax.rsqrt(jnp.mean(y * y, axis=-1, keepdims=True) + NORM_EPS) * norm_ref[:, vs]
                gate = gate_ref[rows, vs]
                o_ref[rows, vs] = (y * (gate * jax.nn.sigmoid(gate))).astype(o_ref.dtype)
            else:
                o_ref[rows, vs] = y


def _ret_tables(log_gamma, C, reverse):
    idx = np.arange(C, dtype=np.float32)
    if reverse:
        diff = idx[None, :] - idx[:, None]
        tri = diff > 0
        q_pow, k_pow = C - idx, idx
    else:
        diff = idx[:, None] - idx[None, :]
        tri = diff >= 0
        q_pow, k_pow = idx + 1.0, C - 1.0 - idx
    lg = log_gamma[:, None, None]
    dec = jnp.where(tri[None], jnp.exp(np.where(tri, diff, 0.0)[None] * lg), 0.0)
    xi = jnp.broadcast_to(jnp.exp(q_pow[None, :, None] * lg), (RET_HEADS, C, RET_QK_DIM))
    zeta = jnp.broadcast_to(jnp.exp(k_pow[None, :, None] * lg), (RET_HEADS, C, RET_QK_DIM))
    gch = jnp.broadcast_to(jnp.exp(C * lg), (RET_HEADS, 1, RET_V_DIM))
    return dec, xi, zeta, gch


def _retention(proj, cos2, sin2, decay_logit, ret_norm, *, B, S, C, cpb):
    T = proj.shape[0]
    R = C * cpb
    NB = S // R
    log_gamma = jax.nn.log_sigmoid(decay_logit.astype(F32))
    cq, ck = COL_RET_Q // RET_QK_WIDTH, COL_RET_K // RET_QK_WIDTH
    cv, cg = COL_RET_V // RET_V_WIDTH, COL_RET_G // RET_V_WIDTH

    def run(reverse, extra_in, extra_specs, out_dtype):
        def blk(n):
            return NB - 1 - n if reverse else n

        tab = lambda shape: pl.BlockSpec(shape, lambda b, n: (0, 0, 0))
        in_specs = [
            pl.BlockSpec((R, RET_QK_WIDTH), lambda b, n: (b * NB + blk(n), cq)),
            pl.BlockSpec((R, RET_QK_WIDTH), lambda b, n: (b * NB + blk(n), ck)),
            pl.BlockSpec((R, RET_V_WIDTH), lambda b, n: (b * NB + blk(n), cv)),
            pl.BlockSpec((R, RET_QK_DIM), lambda b, n: (blk(n), 0)),
            pl.BlockSpec((R, RET_QK_DIM), lambda b, n: (blk(n), 0)),
            tab((RET_HEADS, C, C)), tab((RET_HEADS, C, RET_QK_DIM)), tab((RET_HEADS, C, RET_QK_DIM)),
            tab((RET_HEADS, 1, RET_V_DIM)),
        ] + extra_specs(blk)
        tables = _ret_tables(log_gamma[1 if reverse else 0], C, reverse)
        return pl.pallas_call(
            functools.partial(_ret_body, C=C, cpb=cpb, reverse=reverse),
            grid=(B, NB),
            in_specs=in_specs,
            out_specs=pl.BlockSpec((R, RET_V_WIDTH), lambda b, n: (b * NB + blk(n), 0)),
            out_shape=jax.ShapeDtypeStruct((T, RET_V_WIDTH), out_dtype),
            scratch_shapes=[pltpu.VMEM((RET_HEADS, RET_QK_DIM, RET_V_DIM), F32)],
            compiler_params=_params("parallel", "arbitrary"),
            name="ret_bwd" if reverse else "ret_fwd",
        )(proj, proj, proj, cos2, sin2, *tables, *extra_in)

    fwd = run(False, (), lambda blk: [], F32)
    return run(
        True, (fwd, proj, ret_norm),
        lambda blk: [pl.BlockSpec((R, RET_V_WIDTH), lambda b, n: (b * NB + blk(n), 0)),
                     pl.BlockSpec((R, RET_V_WIDTH), lambda b, n: (b * NB + blk(n), cg)),
                     pl.BlockSpec((1, RET_V_WIDTH), lambda b, n: (0, 0))],
        BF16)


ATTN_SUB = 128


def _attn_body(q_ref, kc_ref, kp_ref, kn_ref, vc_ref, vp_ref, vn_ref, bias_ref, qg_ref, kg_ref,
               o_ref, l_ref, *, tq, L):
    n = pl.program_id(2)
    nk = ATTN_SUB + 2 * ATTN_RADIUS
    row = lax.broadcasted_iota(jnp.int32, (ATTN_SUB, nk), 0)
    col = lax.broadcasted_iota(jnp.int32, (ATTN_SUB, nk), 1)
    band = jnp.abs(col - ATTN_RADIUS - row) <= ATTN_RADIUS
    for h in range(ATTN_HEADS_PER_GROUP):
        cs = slice(h * HEAD_DIM, (h + 1) * HEAD_DIM)
        q = (_rms(q_ref[:, cs], qg_ref[...]) * (HEAD_DIM ** -0.5)).astype(BF16)
        kext = jnp.concatenate([kp_ref[:, cs], kc_ref[:, cs], kn_ref[:, cs]], axis=0)
        kext = _rms(kext, kg_ref[...]).astype(BF16)
        vext = jnp.concatenate([vp_ref[:, cs], vc_ref[:, cs], vn_ref[:, cs]], axis=0).astype(BF16)
        for s in range(tq // ATTN_SUB):
            r0 = s * ATTN_SUB
            kpos = n * tq + r0 - ATTN_RADIUS + col
            mask = band & (kpos >= 0) & (kpos < L)
            logits = lax.dot_general(q[r0:r0 + ATTN_SUB], kext[r0:r0 + nk], (((1,), (1,)), ((), ())),
                                     preferred_element_type=F32) + bias_ref[h]
            logits = jnp.where(mask, logits, NEG_INF)
            m = jnp.max(logits, axis=-1, keepdims=True)
            e = jnp.exp(logits - m)
            ssum = jnp.sum(e, axis=-1, keepdims=True)
            o = jnp.dot(e.astype(BF16), vext[r0:r0 + nk], preferred_element_type=F32) / ssum
            o_ref[r0:r0 + ATTN_SUB, cs] = o
            l_ref[r0:r0 + ATTN_SUB, cs] = jnp.broadcast_to(m + jnp.log(ssum), (ATTN_SUB, HEAD_DIM))


def _attn_group(proj, bias, q_norm, k_norm, *, g, B, S, tq):
    dil = ATTN_PATTERNS[g][1]
    L = S // dil
    tq = min(tq, L)
    projv = proj.reshape(B, L, dil * IN_WIDTH)
    gw = ATTN_GROUP_WIDTH
    nin = IN_WIDTH // gw
    cq, ck, cv = COL_ATT_Q // gw + g, COL_ATT_K // gw + g, COL_ATT_V // gw + g
    hb = tq // ATTN_RADIUS
    last = L // ATTN_RADIUS - 1

    def cur(c):
        return pl.BlockSpec((None, tq, gw), lambda b, r, n: (b, n, r * nin + c))

    def prev(c):
        return pl.BlockSpec((None, ATTN_RADIUS, gw),
                            lambda b, r, n: (b, jnp.maximum(n * hb - 1, 0), r * nin + c))

    def nxt(c):
        return pl.BlockSpec((None, ATTN_RADIUS, gw),
                            lambda b, r, n: (b, jnp.minimum((n + 1) * hb, last), r * nin + c))

    gain = pl.BlockSpec((1, HEAD_DIM), lambda b, r, n: (0, 0))
    out_spec = pl.BlockSpec((None, tq, gw), lambda b, r, n: (b, n, r))
    out_shape = jax.ShapeDtypeStruct((B, L, dil * gw), F32)
    return pl.pallas_call(
        functools.partial(_attn_body, tq=tq, L=L),
        grid=(B, dil, L // tq),
        in_specs=[cur(cq), cur(ck), prev(ck), nxt(ck), cur(cv), prev(cv), nxt(cv),
                  pl.BlockSpec(bias.shape, lambda b, r, n: (0, 0, 0)), gain, gain],
        out_specs=[out_spec, out_spec],
        out_shape=[out_shape, out_shape],
        compiler_params=_params("parallel", "parallel", "arbitrary"),
        name=f"attn_g{g}",
    )(projv, projv, projv, projv, projv, projv, projv, bias, q_norm, k_norm)


def _t5_bucket(rel):
    half = REL_BUCKETS // 2
    exact = half // 2
    offset = np.where(rel > 0, half, 0)
    n = np.abs(rel)
    large = exact + (np.log(np.maximum(n, 1) / exact) / np.log(REL_MAX_DISTANCE / exact)
                     * (half - exact)).astype(np.int32)
    large = np.minimum(large, half - 1)
    return (offset + np.where(n < exact, n, large)).astype(np.int32)


def _attn_bias(rel_bias, g):
    dil = ATTN_PATTERNS[g][1]
    nk = ATTN_SUB + 2 * ATTN_RADIUS
    rel = np.arange(nk)[None, :] - ATTN_RADIUS - np.arange(ATTN_SUB)[:, None]
    table = rel_bias[:, g * ATTN_HEADS_PER_GROUP:(g + 1) * ATTN_HEADS_PER_GROUP]
    return jnp.take(table, _t5_bucket(rel * dil), axis=0).transpose(2, 0, 1).astype(F32)


def _attn_merge_body(o0_ref, o1_ref, o2_ref, l0_ref, l1_ref, l2_ref, y_ref):
    o_refs = (o0_ref, o1_ref, o2_ref)
    l_refs = (l0_ref, l1_ref, l2_ref)
    ls = [r[...] for r in l_refs]
    m = jnp.maximum(jnp.maximum(ls[0], ls[1]), ls[2])
    ws = [jnp.exp(l - m) for l in ls]
    tot = ws[0] + ws[1] + ws[2]
    for g in range(len(ATTN_PATTERNS)):
        cs = slice(g * ATTN_GROUP_WIDTH, (g + 1) * ATTN_GROUP_WIDTH)
        y_ref[:, cs] = (o_refs[g][...] * (ws[g] / tot)).astype(y_ref.dtype)


def _attn_merge(outs, lses, *, tm):
    T = outs[0].shape[0]
    gw = ATTN_GROUP_WIDTH
    specs = [pl.BlockSpec((tm, gw), lambda i: (i, 0)) for _ in ATTN_PATTERNS]
    return pl.pallas_call(
        _attn_merge_body,
        grid=(T // tm,),
        in_specs=specs + specs,
        out_specs=pl.BlockSpec((tm, ATTN_WIDTH), lambda i: (i, 0)),
        out_shape=jax.ShapeDtypeStruct((T, ATTN_WIDTH), BF16),
        compiler_params=_params("parallel"),
        name="attn_merge",
    )(*outs, *lses)


def _attention(proj, rel_bias, q_norm, k_norm, *, B, S, tq, tm):
    T = proj.shape[0]
    outs, lses = [], []
    for g in range(len(ATTN_PATTERNS)):
        o, l = _attn_group(proj, _attn_bias(rel_bias, g), q_norm, k_norm, g=g, B=B, S=S, tq=tq)
        outs.append(o.reshape(T, ATTN_GROUP_WIDTH))
        lses.append(l.reshape(T, ATTN_GROUP_WIDTH))
    return _attn_merge(outs, lses, tm=tm)


def _merge_body(h_ref, g_ref, yp_ref, ya_ref, yr_ref, yc_ref,
                wg0_ref, wg1_ref, wg2_ref, wg3_ref, bg0_ref, bg1_ref, bg2_ref, bg3_ref,
                wp_ref, wa_ref, wr_ref, wc_ref, wo_ref, o_ref, u_ref, acc_ref):
    j = pl.program_id(1)

    @pl.when(j == 0)
    def _():
        u_ref[...] = _rms(h_ref[...], g_ref[...]).astype(BF16)
        acc_ref[...] = jnp.zeros_like(acc_ref)

    u = u_ref[...]
    merged = None
    for y_ref, wb_ref, wg_ref, bg_ref in ((yp_ref, wp_ref, wg0_ref, bg0_ref),
                                          (ya_ref, wa_ref, wg1_ref, bg1_ref),
                                          (yr_ref, wr_ref, wg2_ref, bg2_ref),
                                          (yc_ref, wc_ref, wg3_ref, bg3_ref)):
        gate = jax.nn.sigmoid(jnp.dot(u, wg_ref[...], preferred_element_type=F32) + bg_ref[...])
        term = gate * jnp.dot(y_ref[...], wb_ref[...], preferred_element_type=F32)
        merged = term if merged is None else merged + term
    acc_ref[...] += jnp.dot(merged.astype(BF16), wo_ref[...], preferred_element_type=F32)

    @pl.when(j == pl.num_programs(1) - 1)
    def _():
        o_ref[...] = h_ref[...] + acc_ref[...]


def _merge(h, gain, ys, w_gate, b_gate, w_brs, w_out, *, tm, tn):
    T, D = h.shape
    nb = D // tn
    row = lambda width: pl.BlockSpec((tm, width), lambda i, j: (i, 0))
    gate_specs = [pl.BlockSpec((D, tn), functools.partial(lambda i, j, b: (0, b * nb + j), b=b))
                  for b in range(N_BRANCHES)]
    bias_specs = [pl.BlockSpec((1, tn), functools.partial(lambda i, j, b: (0, b * nb + j), b=b))
                  for b in range(N_BRANCHES)]
    br_specs = [pl.BlockSpec((w.shape[0], tn), lambda i, j: (0, j)) for w in w_brs]
    return pl.pallas_call(
        _merge_body,
        grid=(T // tm, nb),
        in_specs=[row(D), pl.BlockSpec((1, D), lambda i, j: (0, 0))]
        + [row(y.shape[1]) for y in ys] + gate_specs + bias_specs + br_specs
        + [pl.BlockSpec((tn, D), lambda i, j: (j, 0))],
        out_specs=row(D),
        out_shape=jax.ShapeDtypeStruct((T, D), F32),
        scratch_shapes=[pltpu.VMEM((tm, D), BF16), pltpu.VMEM((tm, D), F32)],
        compiler_params=_params("parallel", "arbitrary"),
        name="merge",
    )(h, gain, *ys, *([w_gate] * N_BRANCHES), *([b_gate] * N_BRANCHES), *w_brs, w_out)


def _ple_body(h_ref, g_ref, p_ref, wg_ref, wp_ref, o_ref):
    h = h_ref[...]
    gate = jax.nn.sigmoid(jnp.dot(_rms(h, g_ref[...]).astype(BF16), wg_ref[...],
                                  preferred_element_type=F32))
    ple = jnp.dot(p_ref[...].astype(BF16), wp_ref[...], preferred_element_type=F32)
    o_ref[...] = h + gate * ple


def _ple(h, gain, p, w_gate, w_proj, *, tm):
    T, D = h.shape
    P = p.shape[1]
    return pl.pallas_call(
        _ple_body,
        grid=(T // tm,),
        in_specs=[
            pl.BlockSpec((tm, D), lambda i: (i, 0)),
            pl.BlockSpec((1, D), lambda i: (0, 0)),
            pl.BlockSpec((tm, P), lambda i: (i, 0)),
            pl.BlockSpec((D, D), lambda i: (0, 0)),
            pl.BlockSpec((P, D), lambda i: (0, 0)),
        ],
        out_specs=pl.BlockSpec((tm, D), lambda i: (i, 0)),
        out_shape=jax.ShapeDtypeStruct((T, D), F32),
        compiler_params=_params("parallel"),
        name="ple",
    )(h, gain, p, w_gate, w_proj)


def _permute_w_in(w_in):
    xp, aq, ak, av, rq, rk, rv, rg, cin = jnp.split(
        w_in, np.cumsum((1024, 1536, 1536, 1536, 512, 512, 1024, 1024))[:], axis=-1)
    return jnp.concatenate([cin, xp, rv, rg, aq, ak, av, rq, rk], axis=-1)


def _rope_tables(S):
    half = RET_QK_DIM // 2
    pos = jnp.arange(S, dtype=F32)
    inv = ROPE_BASE ** (-jnp.linspace(0.0, 1.0, half, dtype=F32))
    ang = pos[:, None] * inv[None, :]
    cos, sin = jnp.cos(ang), jnp.sin(ang)
    return jnp.concatenate([cos, cos], axis=-1), jnp.concatenate([-sin, sin], axis=-1)


def kernel(x, p, rel_bias, ffn1_norm, ffn1_w_gate, ffn1_w_up, ffn1_w_down, mix_norm, w_in,
           pool_w, pool_scale, q_norm, k_norm, ret_decay_logit, ret_norm, conv_w, conv_b,
           conv_norm_g, conv_norm_b, w_gate, b_gate, w_br_pool, w_br_attn, w_br_ret, w_br_conv,
           w_out, ffn2_norm, ffn2_w_gate, ffn2_w_up, ffn2_w_down, ple_norm, w_ple_gate, w_ple_proj):
    B, S, D = x.shape
    T = B * S
    depth = p.shape[0]
    cos2, sin2 = _rope_tables(S)
    h = x.reshape(T, D)
    row = lambda v: v.reshape(1, -1)
    bf = lambda w: w.astype(BF16)
    for i in range(depth):
        h = _ffn(h, row(ffn1_norm[i]), bf(ffn1_w_gate[i]), bf(ffn1_w_up[i]), bf(ffn1_w_down[i]),
                 tm=512, tf=512)
        proj = _inproj(h, row(mix_norm[i]), bf(_permute_w_in(w_in[i])), tm=512, tn=512)
        y_pool = _pool(proj, bf(pool_w[i]), row(pool_scale[i]), S=S, tm=512)
        y_attn = _attention(proj, rel_bias, row(q_norm[i]), row(k_norm[i]), B=B, S=S, tq=256, tm=512)
        y_ret = _retention(proj, cos2, sin2, ret_decay_logit[i], row(ret_norm[i]), B=B, S=S, C=128, cpb=4)
        y_conv = _conv(proj, conv_w[i], row(conv_b[i]), row(conv_norm_g[i]), row(conv_norm_b[i]),
                       S=S, tm=256)
        h = _merge(h, row(mix_norm[i]), (y_pool, y_attn, y_ret, y_conv), bf(w_gate[i]), row(b_gate[i]),
                   (bf(w_br_pool[i]), bf(w_br_attn[i]), bf(w_br_ret[i]), bf(w_br_conv[i])),
                   bf(w_out[i]), tm=512, tn=256)
        h = _ffn(h, row(ffn2_norm[i]), bf(ffn2_w_gate[i]), bf(ffn2_w_up[i]), bf(ffn2_w_down[i]),
                 tm=512, tf=512)
        h = _ple(h, row(ple_norm[i]), p[i].reshape(T, -1), bf(w_ple_gate[i]), bf(w_ple_proj[i]), tm=512)
    return h.reshape(B, S, D)
```

```python
import functools

import numpy as np
import jax
import jax.numpy as jnp
from jax import lax
from jax.experimental import pallas as pl
from jax.experimental.pallas import tpu as pltpu

F32 = jnp.float32
BF16 = jnp.bfloat16

D_MODEL = 2048
HEAD_DIM = 128
NORM_EPS = 1e-6
NEG_INF = -1e30
POOL_WINDOWS = (2, 4, 8, 16)
POOL_WIDTH = 1024
POOL_GROUP_WIDTH = 256
POOL_HALO = 16
ATTN_PATTERNS = ((128, 1), (512, 4), (2048, 16))
ATTN_HEADS_PER_GROUP = 4
ATTN_HEADS = 12
ATTN_WIDTH = ATTN_HEADS * HEAD_DIM
ATTN_GROUP_WIDTH = ATTN_HEADS_PER_GROUP * HEAD_DIM
ATTN_RADIUS = 64
REL_BUCKETS = 32
REL_MAX_DISTANCE = 1024
RET_HEADS = 4
RET_QK_DIM = 128
RET_V_DIM = 256
RET_QK_WIDTH = RET_HEADS * RET_QK_DIM
RET_V_WIDTH = RET_HEADS * RET_V_DIM
ROPE_BASE = 10000.0
CONV_WIDTH = 1024
CONV_SIZE = 31
CONV_HALO = 16
N_BRANCHES = 4

IN_WIDTH = 10752
COL_CONV_A = 0
COL_CONV_G = 1024
COL_POOL = 2048
COL_RET_V = 3072
COL_RET_G = 4096
COL_ATT_Q = 5120
COL_ATT_K = 6656
COL_ATT_V = 8192
COL_RET_Q = 9728
COL_RET_K = 10240

VMEM_LIMIT = 60 * 1024 * 1024


def _params(*sem):
    return pltpu.CompilerParams(dimension_semantics=sem, vmem_limit_bytes=VMEM_LIMIT)


def _rms(x, gain):
    return x * lax.rsqrt(jnp.mean(x * x, axis=-1, keepdims=True) + NORM_EPS) * gain


def _ffn_body(h_ref, g_ref, wg_ref, wu_ref, wd_ref, o_ref, xn_ref):
    j = pl.program_id(1)

    @pl.when(j == 0)
    def _():
        xn_ref[...] = _rms(h_ref[...], g_ref[...]).astype(BF16)
        o_ref[...] = jnp.zeros_like(o_ref)

    xn = xn_ref[...]
    a = jnp.dot(xn, wg_ref[...], preferred_element_type=F32)
    b = jnp.dot(xn, wu_ref[...], preferred_element_type=F32)
    mid = (a * jax.nn.sigmoid(a) * b).astype(BF16)
    o_ref[...] += jnp.dot(mid, wd_ref[...], preferred_element_type=F32)

    @pl.when(j == pl.num_programs(1) - 1)
    def _():
        o_ref[...] = h_ref[...] + 0.5 * o_ref[...]


def _ffn(h, gain, wg, wu, wd, *, tm, tf):
    T, D = h.shape
    Fd = wg.shape[1]
    return pl.pallas_call(
        _ffn_body,
        grid=(T // tm, Fd // tf),
        in_specs=[
            pl.BlockSpec((tm, D), lambda i, j: (i, 0)),
            pl.BlockSpec((1, D), lambda i, j: (0, 0)),
            pl.BlockSpec((D, tf), lambda i, j: (0, j)),
            pl.BlockSpec((D, tf), lambda i, j: (0, j)),
            pl.BlockSpec((tf, D), lambda i, j: (j, 0)),
        ],
        out_specs=pl.BlockSpec((tm, D), lambda i, j: (i, 0)),
        out_shape=jax.ShapeDtypeStruct((T, D), F32),
        scratch_shapes=[pltpu.VMEM((tm, D), BF16)],
        compiler_params=_params("parallel", "arbitrary"),
        name="ffn",
    )(h, gain, wg, wu, wd)


def _inproj_body(h_ref, g_ref, w_ref, o_ref, xn_ref):
    @pl.when(pl.program_id(1) == 0)
    def _():
        xn_ref[...] = _rms(h_ref[...], g_ref[...]).astype(BF16)

    o_ref[...] = jnp.dot(xn_ref[...], w_ref[...], preferred_element_type=F32).astype(o_ref.dtype)


def _inproj(h, gain, w, *, tm, tn):
    T, D = h.shape
    N = w.shape[1]
    return pl.pallas_call(
        _inproj_body,
        grid=(T // tm, N // tn),
        in_specs=[
            pl.BlockSpec((tm, D), lambda i, j: (i, 0)),
            pl.BlockSpec((1, D), lambda i, j: (0, 0)),
            pl.BlockSpec((D, tn), lambda i, j: (0, j)),
        ],
        out_specs=pl.BlockSpec((tm, tn), lambda i, j: (i, j)),
        out_shape=jax.ShapeDtypeStruct((T, N), BF16),
        scratch_shapes=[pltpu.VMEM((tm, D), BF16)],
        compiler_params=_params("parallel", "arbitrary"),
        name="inproj",
    )(h, gain, w)


def _pool_body(cur_ref, prev_ref, next_ref, w_ref, sc_ref, o_ref, ext_ref, *, tm, S):
    pos0 = (pl.program_id(0) * tm) % S
    ext_ref[0:POOL_HALO, :] = jnp.where(pos0 == 0, 0.0, prev_ref[...].astype(F32))
    ext_ref[POOL_HALO:POOL_HALO + tm, :] = cur_ref[...].astype(F32)
    ext_ref[POOL_HALO + tm:, :] = jnp.where(pos0 + tm == S, 0.0, next_ref[...].astype(F32))
    t = pos0 + lax.broadcasted_iota(jnp.int32, (tm, 1), 0)
    for g, w in enumerate(POOL_WINDOWS):
        cs = slice(g * POOL_GROUP_WIDTH, (g + 1) * POOL_GROUP_WIDTH)
        tot = ext_ref[POOL_HALO - w // 2:POOL_HALO - w // 2 + tm, cs]
        for k in range(-w // 2 + 1, w // 2):
            tot = tot + ext_ref[POOL_HALO + k:POOL_HALO + k + tm, cs]
        cnt = jnp.minimum(t + w // 2, S) - jnp.maximum(t - w // 2, 0)
        mixed = tot / cnt.astype(F32) - ext_ref[POOL_HALO:POOL_HALO + tm, cs]
        y = jnp.dot(mixed.astype(BF16), w_ref[g], preferred_element_type=F32)
        o_ref[:, cs] = (y * sc_ref[:, cs]).astype(o_ref.dtype)


def _pool(proj, pool_w, pool_scale, *, S, tm):
    T = proj.shape[0]
    cb = COL_POOL // POOL_WIDTH
    hb = tm // POOL_HALO
    last = T // POOL_HALO - 1
    return pl.pallas_call(
        functools.partial(_pool_body, tm=tm, S=S),
        grid=(T // tm,),
        in_specs=[
            pl.BlockSpec((tm, POOL_WIDTH), lambda i: (i, cb)),
            pl.BlockSpec((POOL_HALO, POOL_WIDTH), lambda i: (jnp.maximum(i * hb - 1, 0), cb)),
            pl.BlockSpec((POOL_HALO, POOL_WIDTH), lambda i: (jnp.minimum((i + 1) * hb, last), cb)),
            pl.BlockSpec(pool_w.shape, lambda i: (0, 0, 0)),
            pl.BlockSpec((1, POOL_WIDTH), lambda i: (0, 0)),
        ],
        out_specs=pl.BlockSpec((tm, POOL_WIDTH), lambda i: (i, 0)),
        out_shape=jax.ShapeDtypeStruct((T, POOL_WIDTH), BF16),
        scratch_shapes=[pltpu.VMEM((tm + 2 * POOL_HALO, POOL_WIDTH), F32)],
        compiler_params=_params("parallel"),
        name="pool",
    )(proj, proj, proj, pool_w, pool_scale)


CONV_ROWS = 32


def _conv_body(a_ref, ap_ref, an_ref, g_ref, gp_ref, gn_ref, w_ref, b_ref, ng_ref, nb_ref,
               o_ref, ext_ref, co_ref, *, tm, S):
    pos0 = (pl.program_id(0) * tm) % S

    def glu(x_ref, gate_ref):
        return x_ref[...].astype(F32) * jax.nn.sigmoid(gate_ref[...].astype(F32))

    ext_ref[0:CONV_HALO, :] = jnp.where(pos0 == 0, 0.0, glu(ap_ref, gp_ref))
    ext_ref[CONV_HALO:CONV_HALO + tm, :] = glu(a_ref, g_ref)
    ext_ref[CONV_HALO + tm:, :] = jnp.where(pos0 + tm == S, 0.0, glu(an_ref, gn_ref))
    base = CONV_HALO - CONV_SIZE // 2
    for c in range(tm // CONV_ROWS):
        r0 = c * CONV_ROWS
        acc = ext_ref[r0 + base:r0 + base + CONV_ROWS, :] * w_ref[0:1, :]
        for k in range(1, CONV_SIZE):
            acc = acc + ext_ref[r0 + base + k:r0 + base + k + CONV_ROWS, :] * w_ref[k:k + 1, :]
        co_ref[r0:r0 + CONV_ROWS, :] = acc + b_ref[...]
    hc = co_ref[...]
    mu = jnp.mean(hc, axis=-1, keepdims=True)
    var = jnp.mean(jnp.square(hc - mu), axis=-1, keepdims=True)
    ln = (hc - mu) * lax.rsqrt(var + NORM_EPS) * ng_ref[...] + nb_ref[...]
    o_ref[...] = (ln * jax.nn.sigmoid(ln)).astype(o_ref.dtype)


def _conv(proj, conv_w, conv_b, norm_g, norm_b, *, S, tm):
    T = proj.shape[0]
    ca, cg = COL_CONV_A // CONV_WIDTH, COL_CONV_G // CONV_WIDTH
    hb = tm // CONV_HALO
    last = T // CONV_HALO - 1

    def cur(c):
        return pl.BlockSpec((tm, CONV_WIDTH), lambda i: (i, c))

    def prev(c):
        return pl.BlockSpec((CONV_HALO, CONV_WIDTH), lambda i: (jnp.maximum(i * hb - 1, 0), c))

    def nxt(c):
        return pl.BlockSpec((CONV_HALO, CONV_WIDTH), lambda i: (jnp.minimum((i + 1) * hb, last), c))

    row = pl.BlockSpec((1, CONV_WIDTH), lambda i: (0, 0))
    return pl.pallas_call(
        functools.partial(_conv_body, tm=tm, S=S),
        grid=(T // tm,),
        in_specs=[cur(ca), prev(ca), nxt(ca), cur(cg), prev(cg), nxt(cg),
                  pl.BlockSpec((CONV_SIZE, CONV_WIDTH), lambda i: (0, 0)), row, row, row],
        out_specs=pl.BlockSpec((tm, CONV_WIDTH), lambda i: (i, 0)),
        out_shape=jax.ShapeDtypeStruct((T, CONV_WIDTH), BF16),
        scratch_shapes=[pltpu.VMEM((tm + 2 * CONV_HALO, CONV_WIDTH), F32),
                        pltpu.VMEM((tm, CONV_WIDTH), F32)],
        compiler_params=_params("parallel"),
        name="conv",
    )(proj, proj, proj, proj, proj, proj, conv_w, conv_b, norm_g, norm_b)


def _rotate(x, cos2, sin2):
    return x * cos2 + pltpu.roll(x, RET_QK_DIM // 2, 1) * sin2


def _ret_body(*refs, C, cpb, reverse):
    if reverse:
        (q_ref, k_ref, v_ref, cos_ref, sin_ref, dec_ref, xi_ref, zeta_ref, gch_ref,
         fwd_ref, gate_ref, norm_ref, o_ref, state_ref) = refs
    else:
        (q_ref, k_ref, v_ref, cos_ref, sin_ref, dec_ref, xi_ref, zeta_ref, gch_ref,
         o_ref, state_ref) = refs

    @pl.when(pl.program_id(1) == 0)
    def _():
        state_ref[...] = jnp.zeros_like(state_ref)

    order = range(cpb - 1, -1, -1) if reverse else range(cpb)
    for c in order:
        rows = slice(c * C, (c + 1) * C)
        cos2, sin2 = cos_ref[rows, :], sin_ref[rows, :]
        for h in range(RET_HEADS):
            qk = slice(h * RET_QK_DIM, (h + 1) * RET_QK_DIM)
            vs = slice(h * RET_V_DIM, (h + 1) * RET_V_DIM)
            qr = _rotate(q_ref[rows, qk].astype(F32), cos2, sin2)
            kr = _rotate(k_ref[rows, qk].astype(F32), cos2, sin2) * (RET_QK_DIM ** -0.5)
            vb = v_ref[rows, vs].astype(BF16)
            scores = lax.dot_general(qr.astype(BF16), kr.astype(BF16), (((1,), (1,)), ((), ())),
                                     preferred_element_type=F32) * dec_ref[h]
            inner = jnp.dot(scores.astype(BF16), vb, preferred_element_type=F32)
            state = state_ref[h]
            cross = jnp.dot((qr * xi_ref[h]).astype(BF16), state.astype(BF16),
                            preferred_element_type=F32)
            kv = lax.dot_general((kr * zeta_ref[h]).astype(BF16), vb, (((0,), (0,)), ((), ())),
                                 preferred_element_type=F32)
            state_ref[h] = state * gch_ref[h] + kv
            y = inner + cross
            if reverse:
                y = y + fwd_ref[rows, vs]
                y = y * lax.rsqrt(jnp.mean(y * y, axis=-1, keepdims=True) + NORM_EPS) * norm_ref[:, vs]
                gate = gate_ref[rows, vs].astype(F32)
                o_ref[rows, vs] = (y * (gate * jax.nn.sigmoid(gate))).astype(o_ref.dtype)
            else:
                o_ref[rows, vs] = y


def _ret_tables(log_gamma, C, reverse):
    idx = np.arange(C, dtype=np.float32)
    if reverse:
        diff = idx[None, :] - idx[:, None]
        tri = diff > 0
        q_pow, k_pow = C - idx, idx
    else:
        diff = idx[:, None] - idx[None, :]
        tri = diff >= 0
        q_pow, k_pow = idx + 1.0, C - 1.0 - idx
    lg = log_gamma[:, None, None]
    dec = jnp.where(tri[None], jnp.exp(np.where(tri, diff, 0.0)[None] * lg), 0.0)
    xi = jnp.broadcast_to(jnp.exp(q_pow[None, :, None] * lg), (RET_HEADS, C, RET_QK_DIM))
    zeta = jnp.broadcast_to(jnp.exp(k_pow[None, :, None] * lg), (RET_HEADS, C, RET_QK_DIM))
    gch = jnp.broadcast_to(jnp.exp(C * lg), (RET_HEADS, 1, RET_V_DIM))
    return dec, xi, zeta, gch


def _retention(proj, cos2, sin2, decay_logit, ret_norm, *, B, S, C, cpb):
    T = proj.shape[0]
    R = C * cpb
    NB = S // R
    log_gamma = jax.nn.log_sigmoid(decay_logit.astype(F32))
    cq, ck = COL_RET_Q // RET_QK_WIDTH, COL_RET_K // RET_QK_WIDTH
    cv, cg = COL_RET_V // RET_V_WIDTH, COL_RET_G // RET_V_WIDTH

    def run(reverse, extra_in, extra_specs, out_dtype):
        def blk(n):
            return NB - 1 - n if reverse else n

        tab = lambda shape: pl.BlockSpec(shape, lambda b, n: (0, 0, 0))
        in_specs = [
            pl.BlockSpec((R, RET_QK_WIDTH), lambda b, n: (b * NB + blk(n), cq)),
            pl.BlockSpec((R, RET_QK_WIDTH), lambda b, n: (b * NB + blk(n), ck)),
            pl.BlockSpec((R, RET_V_WIDTH), lambda b, n: (b * NB + blk(n), cv)),
            pl.BlockSpec((R, RET_QK_DIM), lambda b, n: (blk(n), 0)),
            pl.BlockSpec((R, RET_QK_DIM), lambda b, n: (blk(n), 0)),
            tab((RET_HEADS, C, C)), tab((RET_HEADS, C, RET_QK_DIM)), tab((RET_HEADS, C, RET_QK_DIM)),
            tab((RET_HEADS, 1, RET_V_DIM)),
        ] + extra_specs(blk)
        tables = _ret_tables(log_gamma[1 if reverse else 0], C, reverse)
        return pl.pallas_call(
            functools.partial(_ret_body, C=C, cpb=cpb, reverse=reverse),
            grid=(B, NB),
            in_specs=in_specs,
            out_specs=pl.BlockSpec((R, RET_V_WIDTH), lambda b, n: (b * NB + blk(n), 0)),
            out_shape=jax.ShapeDtypeStruct((T, RET_V_WIDTH), out_dtype),
            scratch_shapes=[pltpu.VMEM((RET_HEADS, RET_QK_DIM, RET_V_DIM), F32)],
            compiler_params=_params("parallel", "arbitrary"),
            name="ret_bwd" if reverse else "ret_fwd",
        )(proj, proj, proj, cos2, sin2, *tables, *extra_in)

    fwd = run(False, (), lambda blk: [], F32)
    return run(
        True, (fwd, proj, ret_norm),
        lambda blk: [pl.BlockSpec((R, RET_V_WIDTH), lambda b, n: (b * NB + blk(n), 0)),
                     pl.BlockSpec((R, RET_V_WIDTH), lambda b, n: (b * NB + blk(n), cg)),
                     pl.BlockSpec((1, RET_V_WIDTH), lambda b, n: (0, 0))],
        BF16)


ATTN_SUB = 128


def _attn_body(q_ref, kc_ref, kp_ref, kn_ref, vc_ref, vp_ref, vn_ref, bias_ref, qg_ref, kg_ref,
               o_ref, l_ref, *, tq, L):
    n = pl.program_id(2)
    nk = ATTN_SUB + 2 * ATTN_RADIUS
    row = lax.broadcasted_iota(jnp.int32, (ATTN_SUB, nk), 0)
    col = lax.broadcasted_iota(jnp.int32, (ATTN_SUB, nk), 1)
    band = jnp.abs(col - ATTN_RADIUS - row) <= ATTN_RADIUS
    for h in range(ATTN_HEADS_PER_GROUP):
        cs = slice(h * HEAD_DIM, (h + 1) * HEAD_DIM)
        q = (_rms(q_ref[:, cs].astype(F32), qg_ref[...]) * (HEAD_DIM ** -0.5)).astype(BF16)
        kext = jnp.concatenate([kp_ref[:, cs], kc_ref[:, cs], kn_ref[:, cs]], axis=0)
        kext = _rms(kext.astype(F32), kg_ref[...]).astype(BF16)
        vext = jnp.concatenate([vp_ref[:, cs], vc_ref[:, cs], vn_ref[:, cs]], axis=0)
        for s in range(tq // ATTN_SUB):
            r0 = s * ATTN_SUB
            kpos = n * tq + r0 - ATTN_RADIUS + col
            mask = band & (kpos >= 0) & (kpos < L)
            logits = lax.dot_general(q[r0:r0 + ATTN_SUB], kext[r0:r0 + nk], (((1,), (1,)), ((), ())),
                                     preferred_element_type=F32) + bias_ref[h]
            logits = jnp.where(mask, logits, NEG_INF)
            m = jnp.max(logits, axis=-1, keepdims=True)
            e = jnp.exp(logits - m)
            ssum = jnp.sum(e, axis=-1, keepdims=True)
            o = jnp.dot(e.astype(BF16), vext[r0:r0 + nk], preferred_element_type=F32) / ssum
            o_ref[r0:r0 + ATTN_SUB, cs] = o
            l_ref[r0:r0 + ATTN_SUB, cs] = jnp.broadcast_to(m + jnp.log(ssum), (ATTN_SUB, HEAD_DIM))


def _attn_group(qkv, cols, bias, q_norm, k_norm, *, g, tq):
    B, dil, L, _ = qkv.shape
    tq = min(tq, L)
    gw = ATTN_GROUP_WIDTH
    cq, ck, cv = cols
    hb = tq // ATTN_RADIUS
    last = L // ATTN_RADIUS - 1

    def cur(c):
        return pl.BlockSpec((None, None, tq, gw), lambda b, r, n: (b, r, n, c))

    def prev(c):
        return pl.BlockSpec((None, None, ATTN_RADIUS, gw),
                            lambda b, r, n: (b, r, jnp.maximum(n * hb - 1, 0), c))

    def nxt(c):
        return pl.BlockSpec((None, None, ATTN_RADIUS, gw),
                            lambda b, r, n: (b, r, jnp.minimum((n + 1) * hb, last), c))

    gain = pl.BlockSpec((1, HEAD_DIM), lambda b, r, n: (0, 0))
    out_spec = pl.BlockSpec((None, None, tq, gw), lambda b, r, n: (b, r, n, 0))
    out_shape = jax.ShapeDtypeStruct((B, dil, L, gw), F32)
    return pl.pallas_call(
        functools.partial(_attn_body, tq=tq, L=L),
        grid=(B, dil, L // tq),
        in_specs=[cur(cq), cur(ck), prev(ck), nxt(ck), cur(cv), prev(cv), nxt(cv),
                  pl.BlockSpec(bias.shape, lambda b, r, n: (0, 0, 0)), gain, gain],
        out_specs=[out_spec, out_spec],
        out_shape=[out_shape, out_shape],
        compiler_params=_params("parallel", "parallel", "arbitrary"),
        name=f"attn_g{g}",
    )(qkv, qkv, qkv, qkv, qkv, qkv, qkv, bias, q_norm, k_norm)


def _t5_bucket(rel):
    half = REL_BUCKETS // 2
    exact = half // 2
    offset = np.where(rel > 0, half, 0)
    n = np.abs(rel)
    large = exact + (np.log(np.maximum(n, 1) / exact) / np.log(REL_MAX_DISTANCE / exact)
                     * (half - exact)).astype(np.int32)
    large = np.minimum(large, half - 1)
    return (offset + np.where(n < exact, n, large)).astype(np.int32)


def _attn_bias(rel_bias, g):
    dil = ATTN_PATTERNS[g][1]
    nk = ATTN_SUB + 2 * ATTN_RADIUS
    rel = np.arange(nk)[None, :] - ATTN_RADIUS - np.arange(ATTN_SUB)[:, None]
    table = rel_bias[:, g * ATTN_HEADS_PER_GROUP:(g + 1) * ATTN_HEADS_PER_GROUP]
    return jnp.take(table, _t5_bucket(rel * dil), axis=0).transpose(2, 0, 1).astype(F32)


def _attn_merge_body(o0_ref, o1_ref, o2_ref, l0_ref, l1_ref, l2_ref, y_ref):
    o_refs = (o0_ref, o1_ref, o2_ref)
    l_refs = (l0_ref, l1_ref, l2_ref)
    ls = [r[...] for r in l_refs]
    m = jnp.maximum(jnp.maximum(ls[0], ls[1]), ls[2])
    ws = [jnp.exp(l - m) for l in ls]
    tot = ws[0] + ws[1] + ws[2]
    for g in range(len(ATTN_PATTERNS)):
        cs = slice(g * ATTN_GROUP_WIDTH, (g + 1) * ATTN_GROUP_WIDTH)
        y_ref[:, cs] = (o_refs[g][...] * (ws[g] / tot)).astype(y_ref.dtype)


def _attn_merge(outs, lses, *, tm):
    T = outs[0].shape[0]
    gw = ATTN_GROUP_WIDTH
    specs = [pl.BlockSpec((tm, gw), lambda i: (i, 0)) for _ in ATTN_PATTERNS]
    return pl.pallas_call(
        _attn_merge_body,
        grid=(T // tm,),
        in_specs=specs + specs,
        out_specs=pl.BlockSpec((tm, ATTN_WIDTH), lambda i: (i, 0)),
        out_shape=jax.ShapeDtypeStruct((T, ATTN_WIDTH), BF16),
        compiler_params=_params("parallel"),
        name="attn_merge",
    )(*outs, *lses)


def _attention(proj, rel_bias, q_norm, k_norm, *, B, S, tq, tm):
    T = proj.shape[0]
    gw = ATTN_GROUP_WIDTH
    outs, lses = [], []
    for g, (_, dil) in enumerate(ATTN_PATTERNS):
        L = S // dil
        starts = [c + g * gw for c in (COL_ATT_Q, COL_ATT_K, COL_ATT_V)]
        if dil == 1:
            qkv, cols = proj.reshape(B, 1, S, IN_WIDTH), [c // gw for c in starts]
        else:
            qkv = jnp.concatenate([proj[:, c:c + gw] for c in starts], axis=-1)
            qkv, cols = qkv.reshape(B, L, dil, 3 * gw).transpose(0, 2, 1, 3), [0, 1, 2]
        o, l = _attn_group(qkv, cols, _attn_bias(rel_bias, g), q_norm, k_norm, g=g, tq=tq)
        outs.append(o.transpose(0, 2, 1, 3).reshape(T, gw))
        lses.append(l.transpose(0, 2, 1, 3).reshape(T, gw))
    return _attn_merge(outs, lses, tm=tm)


def _merge_body(h_ref, g_ref, yp_ref, ya_ref, yr_ref, yc_ref,
                wg0_ref, wg1_ref, wg2_ref, wg3_ref, bg0_ref, bg1_ref, bg2_ref, bg3_ref,
                wp_ref, wa_ref, wr_ref, wc_ref, wo_ref, o_ref, u_ref, acc_ref):
    j = pl.program_id(1)

    @pl.when(j == 0)
    def _():
        u_ref[...] = _rms(h_ref[...], g_ref[...]).astype(BF16)
        acc_ref[...] = jnp.zeros_like(acc_ref)

    u = u_ref[...]
    merged = None
    for y_ref, wb_ref, wg_ref, bg_ref in ((yp_ref, wp_ref, wg0_ref, bg0_ref),
                                          (ya_ref, wa_ref, wg1_ref, bg1_ref),
                                          (yr_ref, wr_ref, wg2_ref, bg2_ref),
                                          (yc_ref, wc_ref, wg3_ref, bg3_ref)):
        gate = jax.nn.sigmoid(jnp.dot(u, wg_ref[...], preferred_element_type=F32) + bg_ref[...])
        term = gate * jnp.dot(y_ref[...], wb_ref[...], preferred_element_type=F32)
        merged = term if merged is None else merged + term
    acc_ref[...] += jnp.dot(merged.astype(BF16), wo_ref[...], preferred_element_type=F32)

    @pl.when(j == pl.num_programs(1) - 1)
    def _():
        o_ref[...] = h_ref[...] + acc_ref[...]


def _merge(h, gain, ys, w_gate, b_gate, w_brs, w_out, *, tm, tn):
    T, D = h.shape
    nb = D // tn
    row = lambda width: pl.BlockSpec((tm, width), lambda i, j: (i, 0))
    gate_specs = [pl.BlockSpec((D, tn), functools.partial(lambda i, j, b: (0, b * nb + j), b=b))
                  for b in range(N_BRANCHES)]
    bias_specs = [pl.BlockSpec((1, tn), functools.partial(lambda i, j, b: (0, b * nb + j), b=b))
                  for b in range(N_BRANCHES)]
    br_specs = [pl.BlockSpec((w.shape[0], tn), lambda i, j: (0, j)) for w in w_brs]
    return pl.pallas_call(
        _merge_body,
        grid=(T // tm, nb),
        in_specs=[row(D), pl.BlockSpec((1, D), lambda i, j: (0, 0))]
        + [row(y.shape[1]) for y in ys] + gate_specs + bias_specs + br_specs
        + [pl.BlockSpec((tn, D), lambda i, j: (j, 0))],
        out_specs=row(D),
        out_shape=jax.ShapeDtypeStruct((T, D), F32),
        scratch_shapes=[pltpu.VMEM((tm, D), BF16), pltpu.VMEM((tm, D), F32)],
        compiler_params=_params("parallel", "arbitrary"),
        name="merge",
    )(h, gain, *ys, *([w_gate] * N_BRANCHES), *([b_gate] * N_BRANCHES), *w_brs, w_out)


def _ple_body(h_ref, g_ref, p_ref, wg_ref, wp_ref, o_ref):
    h = h_ref[...]
    gate = jax.nn.sigmoid(jnp.dot(_rms(h, g_ref[...]).astype(BF16), wg_ref[...],
                                  preferred_element_type=F32))
    ple = jnp.dot(p_ref[...].astype(BF16), wp_ref[...], preferred_element_type=F32)
    o_ref[...] = h + gate * ple


def _ple(h, gain, p, w_gate, w_proj, *, tm):
    T, D = h.shape
    P = p.shape[1]
    return pl.pallas_call(
        _ple_body,
        grid=(T // tm,),
        in_specs=[
            pl.BlockSpec((tm, D), lambda i: (i, 0)),
            pl.BlockSpec((1, D), lambda i: (0, 0)),
            pl.BlockSpec((tm, P), lambda i: (i, 0)),
            pl.BlockSpec((D, D), lambda i: (0, 0)),
            pl.BlockSpec((P, D), lambda i: (0, 0)),
        ],
        out_specs=pl.BlockSpec((tm, D), lambda i: (i, 0)),
        out_shape=jax.ShapeDtypeStruct((T, D), F32),
        compiler_params=_params("parallel"),
        name="ple",
    )(h, gain, p, w_gate, w_proj)


def _permute_w_in(w_in):
    xp, aq, ak, av, rq, rk, rv, rg, cin = jnp.split(
        w_in, np.cumsum((1024, 1536, 1536, 1536, 512, 512, 1024, 1024))[:], axis=-1)
    return jnp.concatenate([cin, xp, rv, rg, aq, ak, av, rq, rk], axis=-1)


def _rope_tables(S):
    half = RET_QK_DIM // 2
    pos = jnp.arange(S, dtype=F32)
    inv = ROPE_BASE ** (-jnp.linspace(0.0, 1.0, half, dtype=F32))
    ang = pos[:, None] * inv[None, :]
    cos, sin = jnp.cos(ang), jnp.sin(ang)
    return jnp.concatenate([cos, cos], axis=-1), jnp.concatenate([-sin, sin], axis=-1)


def kernel(x, p, rel_bias, ffn1_norm, ffn1_w_gate, ffn1_w_up, ffn1_w_down, mix_norm, w_in,
           pool_w, pool_scale, q_norm, k_norm, ret_decay_logit, ret_norm, conv_w, conv_b,
           conv_norm_g, conv_norm_b, w_gate, b_gate, w_br_pool, w_br_attn, w_br_ret, w_br_conv,
           w_out, ffn2_norm, ffn2_w_gate, ffn2_w_up, ffn2_w_down, ple_norm, w_ple_gate, w_ple_proj):
    B, S, D = x.shape
    T = B * S
    depth = p.shape[0]
    cos2, sin2 = _rope_tables(S)
    h = x.reshape(T, D)
    row = lambda v: v.reshape(1, -1)
    bf = lambda w: w.astype(BF16)
    for i in range(depth):
        h = _ffn(h, row(ffn1_norm[i]), bf(ffn1_w_gate[i]), bf(ffn1_w_up[i]), bf(ffn1_w_down[i]),
                 tm=1024, tf=512)
        proj = _inproj(h, row(mix_norm[i]), bf(_permute_w_in(w_in[i])), tm=1024, tn=1536)
        y_pool = _pool(proj, bf(pool_w[i]), row(pool_scale[i]), S=S, tm=512)
        y_attn = _attention(proj, rel_bias, row(q_norm[i]), row(k_norm[i]), B=B, S=S, tq=512, tm=512)
        y_ret = _retention(proj, cos2, sin2, ret_decay_logit[i], row(ret_norm[i]), B=B, S=S, C=128, cpb=4)
        y_conv = _conv(proj, conv_w[i], row(conv_b[i]), row(conv_norm_g[i]), row(conv_norm_b[i]),
                       S=S, tm=256)
        h = _merge(h, row(mix_norm[i]), (y_pool, y_attn, y_ret, y_conv), bf(w_gate[i]), row(b_gate[i]),
                   (bf(w_br_pool[i]), bf(w_br_attn[i]), bf(w_br_ret[i]), bf(w_br_conv[i])),
                   bf(w_out[i]), tm=512, tn=256)
        h = _ffn(h, row(ffn2_norm[i]), bf(ffn2_w_gate[i]), bf(ffn2_w_up[i]), bf(ffn2_w_down[i]),
                 tm=1024, tf=512)
        h = _ple(h, row(ple_norm[i]), p[i].reshape(T, -1), bf(w_ple_gate[i]), bf(w_ple_proj[i]), tm=512)
    return h.reshape(B, S, D)
```

```python
import functools

import numpy as np
import jax
import jax.numpy as jnp
from jax import lax
from jax.experimental import pallas as pl
from jax.experimental.pallas import tpu as pltpu

F32 = jnp.float32
BF16 = jnp.bfloat16

D_MODEL = 2048
HEAD_DIM = 128
NORM_EPS = 1e-6
NEG_INF = -1e30
POOL_WINDOWS = (2, 4, 8, 16)
POOL_WIDTH = 1024
POOL_GROUP_WIDTH = 256
POOL_HALO = 16
ATTN_PATTERNS = ((128, 1), (512, 4), (2048, 16))
ATTN_HEADS_PER_GROUP = 4
ATTN_HEADS = 12
ATTN_WIDTH = ATTN_HEADS * HEAD_DIM
ATTN_GROUP_WIDTH = ATTN_HEADS_PER_GROUP * HEAD_DIM
ATTN_RADIUS = 64
REL_BUCKETS = 32
REL_MAX_DISTANCE = 1024
RET_HEADS = 4
RET_QK_DIM = 128
RET_V_DIM = 256
RET_QK_WIDTH = RET_HEADS * RET_QK_DIM
RET_V_WIDTH = RET_HEADS * RET_V_DIM
ROPE_BASE = 10000.0
CONV_WIDTH = 1024
CONV_SIZE = 31
CONV_HALO = 16
N_BRANCHES = 4

IN_WIDTH = 10752
COL_CONV_A = 0
COL_CONV_G = 1024
COL_POOL = 2048
COL_RET_V = 3072
COL_RET_G = 4096
COL_ATT_Q = 5120
COL_ATT_K = 6656
COL_ATT_V = 8192
COL_RET_Q = 9728
COL_RET_K = 10240

VMEM_LIMIT = 60 * 1024 * 1024


def _params(*sem):
    return pltpu.CompilerParams(dimension_semantics=sem, vmem_limit_bytes=VMEM_LIMIT)


def _rms(x, gain):
    return x * lax.rsqrt(jnp.mean(x * x, axis=-1, keepdims=True) + NORM_EPS) * gain


def _ffn_body(h_ref, g_ref, wg_ref, wu_ref, wd_ref, o_ref, xn_ref):
    j = pl.program_id(1)

    @pl.when(j == 0)
    def _():
        xn_ref[...] = _rms(h_ref[...], g_ref[...]).astype(BF16)
        o_ref[...] = jnp.zeros_like(o_ref)

    xn = xn_ref[...]
    a = jnp.dot(xn, wg_ref[...], preferred_element_type=F32)
    b = jnp.dot(xn, wu_ref[...], preferred_element_type=F32)
    mid = (a * jax.nn.sigmoid(a) * b).astype(BF16)
    o_ref[...] += jnp.dot(mid, wd_ref[...], preferred_element_type=F32)

    @pl.when(j == pl.num_programs(1) - 1)
    def _():
        o_ref[...] = h_ref[...] + 0.5 * o_ref[...]


def _ffn(h, gain, wg, wu, wd, *, tm, tf):
    T, D = h.shape
    Fd = wg.shape[1]
    return pl.pallas_call(
        _ffn_body,
        grid=(T // tm, Fd // tf),
        in_specs=[
            pl.BlockSpec((tm, D), lambda i, j: (i, 0)),
            pl.BlockSpec((1, D), lambda i, j: (0, 0)),
            pl.BlockSpec((D, tf), lambda i, j: (0, j)),
            pl.BlockSpec((D, tf), lambda i, j: (0, j)),
            pl.BlockSpec((tf, D), lambda i, j: (j, 0)),
        ],
        out_specs=pl.BlockSpec((tm, D), lambda i, j: (i, 0)),
        out_shape=jax.ShapeDtypeStruct((T, D), F32),
        scratch_shapes=[pltpu.VMEM((tm, D), BF16)],
        compiler_params=_params("parallel", "arbitrary"),
        name="ffn",
    )(h, gain, wg, wu, wd)


def _inproj_body(h_ref, g_ref, w_ref, o_ref, xn_ref):
    @pl.when(pl.program_id(1) == 0)
    def _():
        xn_ref[...] = _rms(h_ref[...], g_ref[...]).astype(BF16)

    o_ref[...] = jnp.dot(xn_ref[...], w_ref[...], preferred_element_type=F32).astype(o_ref.dtype)


def _inproj(h, gain, w, *, tm, tn):
    T, D = h.shape
    N = w.shape[1]
    return pl.pallas_call(
        _inproj_body,
        grid=(T // tm, N // tn),
        in_specs=[
            pl.BlockSpec((tm, D), lambda i, j: (i, 0)),
            pl.BlockSpec((1, D), lambda i, j: (0, 0)),
            pl.BlockSpec((D, tn), lambda i, j: (0, j)),
        ],
        out_specs=pl.BlockSpec((tm, tn), lambda i, j: (i, j)),
        out_shape=jax.ShapeDtypeStruct((T, N), BF16),
        scratch_shapes=[pltpu.VMEM((tm, D), BF16)],
        compiler_params=_params("parallel", "arbitrary"),
        name="inproj",
    )(h, gain, w)


def _pool_body(cur_ref, prev_ref, next_ref, w_ref, sc_ref, o_ref, ext_ref, *, tm, S):
    pos0 = (pl.program_id(0) * tm) % S
    ext_ref[0:POOL_HALO, :] = jnp.where(pos0 == 0, 0.0, prev_ref[...].astype(F32))
    ext_ref[POOL_HALO:POOL_HALO + tm, :] = cur_ref[...].astype(F32)
    ext_ref[POOL_HALO + tm:, :] = jnp.where(pos0 + tm == S, 0.0, next_ref[...].astype(F32))
    t = pos0 + lax.broadcasted_iota(jnp.int32, (tm, 1), 0)
    for g, w in enumerate(POOL_WINDOWS):
        cs = slice(g * POOL_GROUP_WIDTH, (g + 1) * POOL_GROUP_WIDTH)
        tot = ext_ref[POOL_HALO - w // 2:POOL_HALO - w // 2 + tm, cs]
        for k in range(-w // 2 + 1, w // 2):
            tot = tot + ext_ref[POOL_HALO + k:POOL_HALO + k + tm, cs]
        cnt = jnp.minimum(t + w // 2, S) - jnp.maximum(t - w // 2, 0)
        mixed = tot / cnt.astype(F32) - ext_ref[POOL_HALO:POOL_HALO + tm, cs]
        y = jnp.dot(mixed.astype(BF16), w_ref[g], preferred_element_type=F32)
        o_ref[:, cs] = (y * sc_ref[:, cs]).astype(o_ref.dtype)


def _pool(proj, pool_w, pool_scale, *, S, tm):
    T = proj.shape[0]
    cb = COL_POOL // POOL_WIDTH
    hb = tm // POOL_HALO
    last = T // POOL_HALO - 1
    return pl.pallas_call(
        functools.partial(_pool_body, tm=tm, S=S),
        grid=(T // tm,),
        in_specs=[
            pl.BlockSpec((tm, POOL_WIDTH), lambda i: (i, cb)),
            pl.BlockSpec((POOL_HALO, POOL_WIDTH), lambda i: (jnp.maximum(i * hb - 1, 0), cb)),
            pl.BlockSpec((POOL_HALO, POOL_WIDTH), lambda i: (jnp.minimum((i + 1) * hb, last), cb)),
            pl.BlockSpec(pool_w.shape, lambda i: (0, 0, 0)),
            pl.BlockSpec((1, POOL_WIDTH), lambda i: (0, 0)),
        ],
        out_specs=pl.BlockSpec((tm, POOL_WIDTH), lambda i: (i, 0)),
        out_shape=jax.ShapeDtypeStruct((T, POOL_WIDTH), BF16),
        scratch_shapes=[pltpu.VMEM((tm + 2 * POOL_HALO, POOL_WIDTH), F32)],
        compiler_params=_params("parallel"),
        name="pool",
    )(proj, proj, proj, pool_w, pool_scale)


CONV_ROWS = 32
CONV_SHIFT_ROWS = 56
SUBLANES = 8


def _conv_body(a_ref, ap_ref, an_ref, g_ref, gp_ref, gn_ref, w_ref, b_ref, ng_ref, nb_ref,
               o_ref, ext_ref, sh_ref, co_ref, *, tm, S):
    pos0 = (pl.program_id(0) * tm) % S

    def glu(x_ref, gate_ref):
        return x_ref[...].astype(F32) * jax.nn.sigmoid(gate_ref[...].astype(F32))

    ext_ref[0:CONV_HALO, :] = jnp.where(pos0 == 0, 0.0, glu(ap_ref, gp_ref))
    ext_ref[CONV_HALO:CONV_HALO + tm, :] = glu(a_ref, g_ref)
    ext_ref[CONV_HALO + tm:, :] = jnp.where(pos0 + tm == S, 0.0, glu(an_ref, gn_ref))
    sh_rows = tm + 2 * CONV_HALO - SUBLANES
    for s in range(1, SUBLANES):
        for c0 in range(0, sh_rows, CONV_SHIFT_ROWS):
            n = min(CONV_SHIFT_ROWS, sh_rows - c0)
            sh_ref[s - 1, c0:c0 + n, :] = ext_ref[c0 + s:c0 + s + n, :]

    def tap(r0, k):
        off = CONV_HALO - CONV_SIZE // 2 + k
        s = off % SUBLANES
        src = ext_ref if s == 0 else sh_ref.at[s - 1]
        return src[r0 + off - s:r0 + off - s + CONV_ROWS, :] * w_ref[k:k + 1, :]

    for c in range(tm // CONV_ROWS):
        r0 = c * CONV_ROWS
        acc = tap(r0, 0)
        for k in range(1, CONV_SIZE):
            acc = acc + tap(r0, k)
        co_ref[r0:r0 + CONV_ROWS, :] = acc + b_ref[...]
    hc = co_ref[...]
    mu = jnp.mean(hc, axis=-1, keepdims=True)
    var = jnp.mean(jnp.square(hc - mu), axis=-1, keepdims=True)
    ln = (hc - mu) * lax.rsqrt(var + NORM_EPS) * ng_ref[...] + nb_ref[...]
    o_ref[...] = (ln * jax.nn.sigmoid(ln)).astype(o_ref.dtype)


def _conv(proj, conv_w, conv_b, norm_g, norm_b, *, S, tm):
    T = proj.shape[0]
    ca, cg = COL_CONV_A // CONV_WIDTH, COL_CONV_G // CONV_WIDTH
    hb = tm // CONV_HALO
    last = T // CONV_HALO - 1

    def cur(c):
        return pl.BlockSpec((tm, CONV_WIDTH), lambda i: (i, c))

    def prev(c):
        return pl.BlockSpec((CONV_HALO, CONV_WIDTH), lambda i: (jnp.maximum(i * hb - 1, 0), c))

    def nxt(c):
        return pl.BlockSpec((CONV_HALO, CONV_WIDTH), lambda i: (jnp.minimum((i + 1) * hb, last), c))

    row = pl.BlockSpec((1, CONV_WIDTH), lambda i: (0, 0))
    return pl.pallas_call(
        functools.partial(_conv_body, tm=tm, S=S),
        grid=(T // tm,),
        in_specs=[cur(ca), prev(ca), nxt(ca), cur(cg), prev(cg), nxt(cg),
                  pl.BlockSpec((CONV_SIZE, CONV_WIDTH), lambda i: (0, 0)), row, row, row],
        out_specs=pl.BlockSpec((tm, CONV_WIDTH), lambda i: (i, 0)),
        out_shape=jax.ShapeDtypeStruct((T, CONV_WIDTH), BF16),
        scratch_shapes=[pltpu.VMEM((tm + 2 * CONV_HALO, CONV_WIDTH), F32),
                        pltpu.VMEM((SUBLANES - 1, tm + 2 * CONV_HALO - SUBLANES, CONV_WIDTH), F32),
                        pltpu.VMEM((tm, CONV_WIDTH), F32)],
        compiler_params=_params("parallel"),
        name="conv",
    )(proj, proj, proj, proj, proj, proj, conv_w, conv_b, norm_g, norm_b)


def _rotate(x, cos2, sin2):
    return x * cos2 + pltpu.roll(x, RET_QK_DIM // 2, 1) * sin2


def _ret_body(*refs, C, cpb, reverse):
    if reverse:
        (q_ref, k_ref, v_ref, cos_ref, sin_ref, dec_ref, xi_ref, zeta_ref, gch_ref,
         fwd_ref, gate_ref, norm_ref, o_ref, state_ref) = refs
    else:
        (q_ref, k_ref, v_ref, cos_ref, sin_ref, dec_ref, xi_ref, zeta_ref, gch_ref,
         o_ref, state_ref) = refs

    @pl.when(pl.program_id(1) == 0)
    def _():
        state_ref[...] = jnp.zeros_like(state_ref)

    order = range(cpb - 1, -1, -1) if reverse else range(cpb)
    for c in order:
        rows = slice(c * C, (c + 1) * C)
        cos2, sin2 = cos_ref[rows, :], sin_ref[rows, :]
        for h in range(RET_HEADS):
            qk = slice(h * RET_QK_DIM, (h + 1) * RET_QK_DIM)
            vs = slice(h * RET_V_DIM, (h + 1) * RET_V_DIM)
            qr = _rotate(q_ref[rows, qk].astype(F32), cos2, sin2)
            kr = _rotate(k_ref[rows, qk].astype(F32), cos2, sin2) * (RET_QK_DIM ** -0.5)
            vb = v_ref[rows, vs].astype(BF16)
            scores = lax.dot_general(qr.astype(BF16), kr.astype(BF16), (((1,), (1,)), ((), ())),
                                     preferred_element_type=F32) * dec_ref[h]
            inner = jnp.dot(scores.astype(BF16), vb, preferred_element_type=F32)
            state = state_ref[h]
            cross = jnp.dot((qr * xi_ref[h]).astype(BF16), state.astype(BF16),
                            preferred_element_type=F32)
            kv = lax.dot_general((kr * zeta_ref[h]).astype(BF16), vb, (((0,), (0,)), ((), ())),
                                 preferred_element_type=F32)
            state_ref[h] = state * gch_ref[h] + kv
            y = inner + cross
            if reverse:
                y = y + fwd_ref[rows, vs]
                y = y * lax.rsqrt(jnp.mean(y * y, axis=-1, keepdims=True) + NORM_EPS) * norm_ref[:, vs]
                gate = gate_ref[rows, vs].astype(F32)
                o_ref[rows, vs] = (y * (gate * jax.nn.sigmoid(gate))).astype(o_ref.dtype)
            else:
                o_ref[rows, vs] = y


def _ret_tables(log_gamma, C, reverse):
    idx = np.arange(C, dtype=np.float32)
    if reverse:
        diff = idx[None, :] - idx[:, None]
        tri = diff > 0
        q_pow, k_pow = C - idx, idx
    else:
        diff = idx[:, None] - idx[None, :]
        tri = diff >= 0
        q_pow, k_pow = idx + 1.0, C - 1.0 - idx
    lg = log_gamma[:, None, None]
    dec = jnp.where(tri[None], jnp.exp(np.where(tri, diff, 0.0)[None] * lg), 0.0)
    xi = jnp.broadcast_to(jnp.exp(q_pow[None, :, None] * lg), (RET_HEADS, C, RET_QK_DIM))
    zeta = jnp.broadcast_to(jnp.exp(k_pow[None, :, None] * lg), (RET_HEADS, C, RET_QK_DIM))
    gch = jnp.broadcast_to(jnp.exp(C * lg), (RET_HEADS, 1, RET_V_DIM))
    return dec, xi, zeta, gch


def _retention(proj, cos2, sin2, decay_logit, ret_norm, *, B, S, C, cpb):
    T = proj.shape[0]
    R = C * cpb
    NB = S // R
    log_gamma = jax.nn.log_sigmoid(decay_logit.astype(F32))
    cq, ck = COL_RET_Q // RET_QK_WIDTH, COL_RET_K // RET_QK_WIDTH
    cv, cg = COL_RET_V // RET_V_WIDTH, COL_RET_G // RET_V_WIDTH

    def run(reverse, extra_in, extra_specs, out_dtype):
        def blk(n):
            return NB - 1 - n if reverse else n

        tab = lambda shape: pl.BlockSpec(shape, lambda b, n: (0, 0, 0))
        in_specs = [
            pl.BlockSpec((R, RET_QK_WIDTH), lambda b, n: (b * NB + blk(n), cq)),
            pl.BlockSpec((R, RET_QK_WIDTH), lambda b, n: (b * NB + blk(n), ck)),
            pl.BlockSpec((R, RET_V_WIDTH), lambda b, n: (b * NB + blk(n), cv)),
            pl.BlockSpec((R, RET_QK_DIM), lambda b, n: (blk(n), 0)),
            pl.BlockSpec((R, RET_QK_DIM), lambda b, n: (blk(n), 0)),
            tab((RET_HEADS, C, C)), tab((RET_HEADS, C, RET_QK_DIM)), tab((RET_HEADS, C, RET_QK_DIM)),
            tab((RET_HEADS, 1, RET_V_DIM)),
        ] + extra_specs(blk)
        tables = _ret_tables(log_gamma[1 if reverse else 0], C, reverse)
        return pl.pallas_call(
            functools.partial(_ret_body, C=C, cpb=cpb, reverse=reverse),
            grid=(B, NB),
            in_specs=in_specs,
            out_specs=pl.BlockSpec((R, RET_V_WIDTH), lambda b, n: (b * NB + blk(n), 0)),
            out_shape=jax.ShapeDtypeStruct((T, RET_V_WIDTH), out_dtype),
            scratch_shapes=[pltpu.VMEM((RET_HEADS, RET_QK_DIM, RET_V_DIM), F32)],
            compiler_params=_params("parallel", "arbitrary"),
            name="ret_bwd" if reverse else "ret_fwd",
        )(proj, proj, proj, cos2, sin2, *tables, *extra_in)

    fwd = run(False, (), lambda blk: [], F32)
    return run(
        True, (fwd, proj, ret_norm),
        lambda blk: [pl.BlockSpec((R, RET_V_WIDTH), lambda b, n: (b * NB + blk(n), 0)),
                     pl.BlockSpec((R, RET_V_WIDTH), lambda b, n: (b * NB + blk(n), cg)),
                     pl.BlockSpec((1, RET_V_WIDTH), lambda b, n: (0, 0))],
        BF16)


ATTN_SUB = 128


def _attn_body(q_ref, kc_ref, kp_ref, kn_ref, vc_ref, vp_ref, vn_ref, bias_ref, qg_ref, kg_ref,
               o_ref, l_ref, *, tq, L):
    n = pl.program_id(2)
    nk = ATTN_SUB + 2 * ATTN_RADIUS
    row = lax.broadcasted_iota(jnp.int32, (ATTN_SUB, nk), 0)
    col = lax.broadcasted_iota(jnp.int32, (ATTN_SUB, nk), 1)
    band = jnp.abs(col - ATTN_RADIUS - row) <= ATTN_RADIUS
    for h in range(ATTN_HEADS_PER_GROUP):
        cs = slice(h * HEAD_DIM, (h + 1) * HEAD_DIM)
        q = (_rms(q_ref[:, cs].astype(F32), qg_ref[...]) * (HEAD_DIM ** -0.5)).astype(BF16)
        kext = jnp.concatenate([kp_ref[:, cs], kc_ref[:, cs], kn_ref[:, cs]], axis=0)
        kext = _rms(kext.astype(F32), kg_ref[...]).astype(BF16)
        vext = jnp.concatenate([vp_ref[:, cs], vc_ref[:, cs], vn_ref[:, cs]], axis=0)
        for s in range(tq // ATTN_SUB):
            r0 = s * ATTN_SUB
            kpos = n * tq + r0 - ATTN_RADIUS + col
            mask = band & (kpos >= 0) & (kpos < L)
            logits = lax.dot_general(q[r0:r0 + ATTN_SUB], kext[r0:r0 + nk], (((1,), (1,)), ((), ())),
                                     preferred_element_type=F32) + bias_ref[h]
            logits = jnp.where(mask, logits, NEG_INF)
            m = jnp.max(logits, axis=-1, keepdims=True)
            e = jnp.exp(logits - m)
            ssum = jnp.sum(e, axis=-1, keepdims=True)
            o = jnp.dot(e.astype(BF16), vext[r0:r0 + nk], preferred_element_type=F32) / ssum
            o_ref[r0:r0 + ATTN_SUB, cs] = o.astype(o_ref.dtype)
            l_ref[r0:r0 + ATTN_SUB, cs] = jnp.broadcast_to(m + jnp.log(ssum), (ATTN_SUB, HEAD_DIM))


def _attn_group(qkv, cols, bias, q_norm, k_norm, *, g, tq):
    B, dil, L, _ = qkv.shape
    tq = min(tq, L)
    gw = ATTN_GROUP_WIDTH
    cq, ck, cv = cols
    hb = tq // ATTN_RADIUS
    last = L // ATTN_RADIUS - 1

    def cur(c):
        return pl.BlockSpec((None, None, tq, gw), lambda b, r, n: (b, r, n, c))

    def prev(c):
        return pl.BlockSpec((None, None, ATTN_RADIUS, gw),
                            lambda b, r, n: (b, r, jnp.maximum(n * hb - 1, 0), c))

    def nxt(c):
        return pl.BlockSpec((None, None, ATTN_RADIUS, gw),
                            lambda b, r, n: (b, r, jnp.minimum((n + 1) * hb, last), c))

    gain = pl.BlockSpec((1, HEAD_DIM), lambda b, r, n: (0, 0))
    out_spec = pl.BlockSpec((None, None, tq, gw), lambda b, r, n: (b, r, n, 0))
    out_shape = jax.ShapeDtypeStruct((B, dil, L, gw), F32)
    return pl.pallas_call(
        functools.partial(_attn_body, tq=tq, L=L),
        grid=(B, dil, L // tq),
        in_specs=[cur(cq), cur(ck), prev(ck), nxt(ck), cur(cv), prev(cv), nxt(cv),
                  pl.BlockSpec(bias.shape, lambda b, r, n: (0, 0, 0)), gain, gain],
        out_specs=[out_spec, out_spec],
        out_shape=[jax.ShapeDtypeStruct(out_shape.shape, BF16), out_shape],
        compiler_params=_params("parallel", "parallel", "arbitrary"),
        name=f"attn_g{g}",
    )(qkv, qkv, qkv, qkv, qkv, qkv, qkv, bias, q_norm, k_norm)


def _t5_bucket(rel):
    half = REL_BUCKETS // 2
    exact = half // 2
    offset = np.where(rel > 0, half, 0)
    n = np.abs(rel)
    large = exact + (np.log(np.maximum(n, 1) / exact) / np.log(REL_MAX_DISTANCE / exact)
                     * (half - exact)).astype(np.int32)
    large = np.minimum(large, half - 1)
    return (offset + np.where(n < exact, n, large)).astype(np.int32)


def _attn_bias(rel_bias, g):
    dil = ATTN_PATTERNS[g][1]
    nk = ATTN_SUB + 2 * ATTN_RADIUS
    span = nk - ATTN_RADIUS - 1
    period = 2 * span + 2
    rel = np.arange(period) - span
    table = rel_bias[:, g * ATTN_HEADS_PER_GROUP:(g + 1) * ATTN_HEADS_PER_GROUP]
    v = jnp.take(table, _t5_bucket(rel * dil), axis=0).T.astype(F32)
    skew = jnp.tile(v, (1, ATTN_SUB))[:, :ATTN_SUB * (period - 1)].reshape(-1, ATTN_SUB, period - 1)
    return skew[:, :, ATTN_SUB - 1:ATTN_SUB - 1 + nk]


def _attn_merge_body(o0_ref, o1_ref, o2_ref, l0_ref, l1_ref, l2_ref, y_ref):
    o_refs = (o0_ref, o1_ref, o2_ref)
    l_refs = (l0_ref, l1_ref, l2_ref)
    ls = [r[...] for r in l_refs]
    m = jnp.maximum(jnp.maximum(ls[0], ls[1]), ls[2])
    ws = [jnp.exp(l - m) for l in ls]
    tot = ws[0] + ws[1] + ws[2]
    for g in range(len(ATTN_PATTERNS)):
        cs = slice(g * ATTN_GROUP_WIDTH, (g + 1) * ATTN_GROUP_WIDTH)
        y_ref[:, cs] = (o_refs[g][...].astype(F32) * (ws[g] / tot)).astype(y_ref.dtype)


def _attn_merge(outs, lses, *, tm):
    T = outs[0].shape[0]
    gw = ATTN_GROUP_WIDTH
    specs = [pl.BlockSpec((tm, gw), lambda i: (i, 0)) for _ in ATTN_PATTERNS]
    return pl.pallas_call(
        _attn_merge_body,
        grid=(T // tm,),
        in_specs=specs + specs,
        out_specs=pl.BlockSpec((tm, ATTN_WIDTH), lambda i: (i, 0)),
        out_shape=jax.ShapeDtypeStruct((T, ATTN_WIDTH), BF16),
        compiler_params=_params("parallel"),
        name="attn_merge",
    )(*outs, *lses)


def _attention(proj, rel_bias, q_norm, k_norm, *, B, S, tq, tm):
    T = proj.shape[0]
    gw = ATTN_GROUP_WIDTH
    outs, lses = [], []
    for g, (_, dil) in enumerate(ATTN_PATTERNS):
        L = S // dil
        starts = [c + g * gw for c in (COL_ATT_Q, COL_ATT_K, COL_ATT_V)]
        if dil == 1:
            qkv, cols = proj.reshape(B, 1, S, IN_WIDTH), [c // gw for c in starts]
        else:
            qkv = jnp.concatenate([proj[:, c:c + gw] for c in starts], axis=-1)
            qkv, cols = qkv.reshape(B, L, dil, 3 * gw).transpose(0, 2, 1, 3), [0, 1, 2]
        o, l = _attn_group(qkv, cols, _attn_bias(rel_bias, g), q_norm, k_norm, g=g, tq=tq)
        outs.append(o.transpose(0, 2, 1, 3).reshape(T, gw))
        lses.append(l.transpose(0, 2, 1, 3).reshape(T, gw))
    return _attn_merge(outs, lses, tm=tm)


def _merge_body(h_ref, g_ref, yp_ref, ya_ref, yr_ref, yc_ref,
                wg0_ref, wg1_ref, wg2_ref, wg3_ref, bg0_ref, bg1_ref, bg2_ref, bg3_ref,
                wp_ref, wa_ref, wr_ref, wc_ref, wo_ref, o_ref, u_ref):
    j = pl.program_id(1)

    @pl.when(j == 0)
    def _():
        u_ref[...] = _rms(h_ref[...], g_ref[...]).astype(BF16)
        o_ref[...] = h_ref[...]

    u = u_ref[...]
    merged = None
    for y_ref, wb_ref, wg_ref, bg_ref in ((yp_ref, wp_ref, wg0_ref, bg0_ref),
                                          (ya_ref, wa_ref, wg1_ref, bg1_ref),
                                          (yr_ref, wr_ref, wg2_ref, bg2_ref),
                                          (yc_ref, wc_ref, wg3_ref, bg3_ref)):
        gate = jax.nn.sigmoid(jnp.dot(u, wg_ref[...], preferred_element_type=F32) + bg_ref[...])
        term = gate * jnp.dot(y_ref[...], wb_ref[...], preferred_element_type=F32)
        merged = term if merged is None else merged + term
    o_ref[...] += jnp.dot(merged.astype(BF16), wo_ref[...], preferred_element_type=F32)


def _merge(h, gain, ys, w_gate, b_gate, w_brs, w_out, *, tm, tn):
    T, D = h.shape
    nb = D // tn
    row = lambda width: pl.BlockSpec((tm, width), lambda i, j: (i, 0))
    gate_specs = [pl.BlockSpec((D, tn), functools.partial(lambda i, j, b: (0, b * nb + j), b=b))
                  for b in range(N_BRANCHES)]
    bias_specs = [pl.BlockSpec((1, tn), functools.partial(lambda i, j, b: (0, b * nb + j), b=b))
                  for b in range(N_BRANCHES)]
    br_specs = [pl.BlockSpec((w.shape[0], tn), lambda i, j: (0, j)) for w in w_brs]
    return pl.pallas_call(
        _merge_body,
        grid=(T // tm, nb),
        in_specs=[row(D), pl.BlockSpec((1, D), lambda i, j: (0, 0))]
        + [row(y.shape[1]) for y in ys] + gate_specs + bias_specs + br_specs
        + [pl.BlockSpec((tn, D), lambda i, j: (j, 0))],
        out_specs=row(D),
        out_shape=jax.ShapeDtypeStruct((T, D), F32),
        scratch_shapes=[pltpu.VMEM((tm, D), BF16)],
        compiler_params=_params("parallel", "arbitrary"),
        name="merge",
    )(h, gain, *ys, *([w_gate] * N_BRANCHES), *([b_gate] * N_BRANCHES), *w_brs, w_out)


def _ple_body(h_ref, g_ref, p_ref, wg_ref, wp_ref, o_ref):
    h = h_ref[...]
    gate = jax.nn.sigmoid(jnp.dot(_rms(h, g_ref[...]).astype(BF16), wg_ref[...],
                                  preferred_element_type=F32))
    ple = jnp.dot(p_ref[...].astype(BF16), wp_ref[...], preferred_element_type=F32)
    o_ref[...] = h + gate * ple


def _ple(h, gain, p, w_gate, w_proj, *, tm):
    T, D = h.shape
    P = p.shape[1]
    return pl.pallas_call(
        _ple_body,
        grid=(T // tm,),
        in_specs=[
            pl.BlockSpec((tm, D), lambda i: (i, 0)),
            pl.BlockSpec((1, D), lambda i: (0, 0)),
            pl.BlockSpec((tm, P), lambda i: (i, 0)),
            pl.BlockSpec((D, D), lambda i: (0, 0)),
            pl.BlockSpec((P, D), lambda i: (0, 0)),
        ],
        out_specs=pl.BlockSpec((tm, D), lambda i: (i, 0)),
        out_shape=jax.ShapeDtypeStruct((T, D), F32),
        compiler_params=_params("parallel"),
        name="ple",
    )(h, gain, p, w_gate, w_proj)


def _permute_w_in(w_in):
    xp, aq, ak, av, rq, rk, rv, rg, cin = jnp.split(
        w_in, np.cumsum((1024, 1536, 1536, 1536, 512, 512, 1024, 1024))[:], axis=-1)
    return jnp.concatenate([cin, xp, rv, rg, aq, ak, av, rq, rk], axis=-1)


def _rope_tables(S):
    half = RET_QK_DIM // 2
    pos = jnp.arange(S, dtype=F32)
    inv = ROPE_BASE ** (-jnp.linspace(0.0, 1.0, half, dtype=F32))
    ang = pos[:, None] * inv[None, :]
    cos, sin = jnp.cos(ang), jnp.sin(ang)
    return jnp.concatenate([cos, cos], axis=-1), jnp.concatenate([-sin, sin], axis=-1)


def kernel(x, p, rel_bias, ffn1_norm, ffn1_w_gate, ffn1_w_up, ffn1_w_down, mix_norm, w_in,
           pool_w, pool_scale, q_norm, k_norm, ret_decay_logit, ret_norm, conv_w, conv_b,
           conv_norm_g, conv_norm_b, w_gate, b_gate, w_br_pool, w_br_attn, w_br_ret, w_br_conv,
           w_out, ffn2_norm, ffn2_w_gate, ffn2_w_up, ffn2_w_down, ple_norm, w_ple_gate, w_ple_proj):
    B, S, D = x.shape
    T = B * S
    depth = p.shape[0]
    cos2, sin2 = _rope_tables(S)
    h = x.reshape(T, D)
    row = lambda v: v.reshape(1, -1)
    bf = lambda w: w.astype(BF16)
    for i in range(depth):
        h = _ffn(h, row(ffn1_norm[i]), bf(ffn1_w_gate[i]), bf(ffn1_w_up[i]), bf(ffn1_w_down[i]),
                 tm=1024, tf=512)
        proj = _inproj(h, row(mix_norm[i]), bf(_permute_w_in(w_in[i])), tm=1024, tn=1536)
        y_pool = _pool(proj, bf(pool_w[i]), row(pool_scale[i]), S=S, tm=512)
        y_attn = _attention(proj, rel_bias, row(q_norm[i]), row(k_norm[i]), B=B, S=S, tq=512, tm=512)
        y_ret = _retention(proj, cos2, sin2, ret_decay_logit[i], row(ret_norm[i]), B=B, S=S, C=128, cpb=4)
        y_conv = _conv(proj, conv_w[i], row(conv_b[i]), row(conv_norm_g[i]), row(conv_norm_b[i]),
                       S=S, tm=256)
        h = _merge(h, row(mix_norm[i]), (y_pool, y_attn, y_ret, y_conv), bf(w_gate[i]), row(b_gate[i]),
                   (bf(w_br_pool[i]), bf(w_br_attn[i]), bf(w_br_ret[i]), bf(w_br_conv[i])),
                   bf(w_out[i]), tm=512, tn=256)
        h = _ffn(h, row(ffn2_norm[i]), bf(ffn2_w_gate[i]), bf(ffn2_w_up[i]), bf(ffn2_w_down[i]),
                 tm=1024, tf=512)
        h = _ple(h, row(ple_norm[i]), p[i].reshape(T, -1), bf(w_ple_gate[i]), bf(w_ple_proj[i]), tm=512)
    return h.reshape(B, S, D)
```

```python
import functools

import numpy as np
import jax
import jax.numpy as jnp
from jax import lax
from jax.experimental import pallas as pl
from jax.experimental.pallas import tpu as pltpu

F32 = jnp.float32
BF16 = jnp.bfloat16

D_MODEL = 2048
HEAD_DIM = 128
NORM_EPS = 1e-6
NEG_INF = -1e30
POOL_WINDOWS = (2, 4, 8, 16)
POOL_WIDTH = 1024
POOL_GROUP_WIDTH = 256
POOL_HALO = 16
ATTN_PATTERNS = ((128, 1), (512, 4), (2048, 16))
ATTN_HEADS_PER_GROUP = 4
ATTN_HEADS = 12
ATTN_WIDTH = ATTN_HEADS * HEAD_DIM
ATTN_GROUP_WIDTH = ATTN_HEADS_PER_GROUP * HEAD_DIM
ATTN_RADIUS = 64
REL_BUCKETS = 32
REL_MAX_DISTANCE = 1024
RET_HEADS = 4
RET_QK_DIM = 128
RET_V_DIM = 256
RET_QK_WIDTH = RET_HEADS * RET_QK_DIM
RET_V_WIDTH = RET_HEADS * RET_V_DIM
ROPE_BASE = 10000.0
CONV_WIDTH = 1024
CONV_SIZE = 31
CONV_HALO = 16
N_BRANCHES = 4

IN_WIDTH = 10752
COL_CONV_A = 0
COL_CONV_G = 1024
COL_POOL = 2048
COL_RET_V = 3072
COL_RET_G = 4096
COL_ATT = 5120
COL_RET_Q = 9728
COL_RET_K = 10240

VMEM_LIMIT = 60 * 1024 * 1024


def _params(*sem):
    return pltpu.CompilerParams(dimension_semantics=sem, vmem_limit_bytes=VMEM_LIMIT)


def _rms(x, gain):
    return x * lax.rsqrt(jnp.mean(x * x, axis=-1, keepdims=True) + NORM_EPS) * gain


def _ffn_body(h_ref, g_ref, wg_ref, wu_ref, wd_ref, o_ref, xn_ref):
    j = pl.program_id(1)

    @pl.when(j == 0)
    def _():
        xn_ref[...] = _rms(h_ref[...], g_ref[...]).astype(BF16)
        o_ref[...] = jnp.zeros_like(o_ref)

    xn = xn_ref[...]
    a = jnp.dot(xn, wg_ref[...], preferred_element_type=F32)
    b = jnp.dot(xn, wu_ref[...], preferred_element_type=F32)
    mid = (a * jax.nn.sigmoid(a) * b).astype(BF16)
    o_ref[...] += jnp.dot(mid, wd_ref[...], preferred_element_type=F32)

    @pl.when(j == pl.num_programs(1) - 1)
    def _():
        o_ref[...] = h_ref[...] + 0.5 * o_ref[...]


def _ffn(h, gain, wg, wu, wd, *, tm, tf):
    T, D = h.shape
    Fd = wg.shape[1]
    return pl.pallas_call(
        _ffn_body,
        grid=(T // tm, Fd // tf),
        in_specs=[
            pl.BlockSpec((tm, D), lambda i, j: (i, 0)),
            pl.BlockSpec((1, D), lambda i, j: (0, 0)),
            pl.BlockSpec((D, tf), lambda i, j: (0, j)),
            pl.BlockSpec((D, tf), lambda i, j: (0, j)),
            pl.BlockSpec((tf, D), lambda i, j: (j, 0)),
        ],
        out_specs=pl.BlockSpec((tm, D), lambda i, j: (i, 0)),
        out_shape=jax.ShapeDtypeStruct((T, D), F32),
        scratch_shapes=[pltpu.VMEM((tm, D), BF16)],
        compiler_params=_params("parallel", "arbitrary"),
        name="ffn",
    )(h, gain, wg, wu, wd)


def _inproj_body(h_ref, g_ref, w_ref, o_ref, xn_ref):
    @pl.when(pl.program_id(1) == 0)
    def _():
        xn_ref[...] = _rms(h_ref[...], g_ref[...]).astype(BF16)

    o_ref[...] = jnp.dot(xn_ref[...], w_ref[...], preferred_element_type=F32).astype(o_ref.dtype)


def _inproj(h, gain, w, *, tm, tn):
    T, D = h.shape
    N = w.shape[1]
    return pl.pallas_call(
        _inproj_body,
        grid=(T // tm, N // tn),
        in_specs=[
            pl.BlockSpec((tm, D), lambda i, j: (i, 0)),
            pl.BlockSpec((1, D), lambda i, j: (0, 0)),
            pl.BlockSpec((D, tn), lambda i, j: (0, j)),
        ],
        out_specs=pl.BlockSpec((tm, tn), lambda i, j: (i, j)),
        out_shape=jax.ShapeDtypeStruct((T, N), BF16),
        scratch_shapes=[pltpu.VMEM((tm, D), BF16)],
        compiler_params=_params("parallel", "arbitrary"),
        name="inproj",
    )(h, gain, w)


def _pool_body(cur_ref, prev_ref, next_ref, w_ref, sc_ref, o_ref, ext_ref, *, tm, S):
    pos0 = (pl.program_id(0) * tm) % S
    ext_ref[0:POOL_HALO, :] = jnp.where(pos0 == 0, 0.0, prev_ref[...].astype(F32))
    ext_ref[POOL_HALO:POOL_HALO + tm, :] = cur_ref[...].astype(F32)
    ext_ref[POOL_HALO + tm:, :] = jnp.where(pos0 + tm == S, 0.0, next_ref[...].astype(F32))
    t = pos0 + lax.broadcasted_iota(jnp.int32, (tm, 1), 0)
    for g, w in enumerate(POOL_WINDOWS):
        cs = slice(g * POOL_GROUP_WIDTH, (g + 1) * POOL_GROUP_WIDTH)
        tot = ext_ref[POOL_HALO - w // 2:POOL_HALO - w // 2 + tm, cs]
        for k in range(-w // 2 + 1, w // 2):
            tot = tot + ext_ref[POOL_HALO + k:POOL_HALO + k + tm, cs]
        cnt = jnp.minimum(t + w // 2, S) - jnp.maximum(t - w // 2, 0)
        mixed = tot / cnt.astype(F32) - ext_ref[POOL_HALO:POOL_HALO + tm, cs]
        y = jnp.dot(mixed.astype(BF16), w_ref[g], preferred_element_type=F32)
        o_ref[:, cs] = (y * sc_ref[:, cs]).astype(o_ref.dtype)


def _pool(proj, pool_w, pool_scale, *, S, tm):
    T = proj.shape[0]
    cb = COL_POOL // POOL_WIDTH
    hb = tm // POOL_HALO
    last = T // POOL_HALO - 1
    return pl.pallas_call(
        functools.partial(_pool_body, tm=tm, S=S),
        grid=(T // tm,),
        in_specs=[
            pl.BlockSpec((tm, POOL_WIDTH), lambda i: (i, cb)),
            pl.BlockSpec((POOL_HALO, POOL_WIDTH), lambda i: (jnp.maximum(i * hb - 1, 0), cb)),
            pl.BlockSpec((POOL_HALO, POOL_WIDTH), lambda i: (jnp.minimum((i + 1) * hb, last), cb)),
            pl.BlockSpec(pool_w.shape, lambda i: (0, 0, 0)),
            pl.BlockSpec((1, POOL_WIDTH), lambda i: (0, 0)),
        ],
        out_specs=pl.BlockSpec((tm, POOL_WIDTH), lambda i: (i, 0)),
        out_shape=jax.ShapeDtypeStruct((T, POOL_WIDTH), BF16),
        scratch_shapes=[pltpu.VMEM((tm + 2 * POOL_HALO, POOL_WIDTH), F32)],
        compiler_params=_params("parallel"),
        name="pool",
    )(proj, proj, proj, pool_w, pool_scale)


CONV_ROWS = 32
CONV_SHIFT_ROWS = 56
SUBLANES = 8


def _conv_body(a_ref, ap_ref, an_ref, g_ref, gp_ref, gn_ref, w_ref, b_ref, ng_ref, nb_ref,
               o_ref, ext_ref, sh_ref, co_ref, *, tm, S):
    pos0 = (pl.program_id(0) * tm) % S

    def glu(x_ref, gate_ref):
        return x_ref[...].astype(F32) * jax.nn.sigmoid(gate_ref[...].astype(F32))

    ext_ref[0:CONV_HALO, :] = jnp.where(pos0 == 0, 0.0, glu(ap_ref, gp_ref))
    ext_ref[CONV_HALO:CONV_HALO + tm, :] = glu(a_ref, g_ref)
    ext_ref[CONV_HALO + tm:, :] = jnp.where(pos0 + tm == S, 0.0, glu(an_ref, gn_ref))
    sh_rows = tm + 2 * CONV_HALO - SUBLANES
    for s in range(1, SUBLANES):
        for c0 in range(0, sh_rows, CONV_SHIFT_ROWS):
            n = min(CONV_SHIFT_ROWS, sh_rows - c0)
            sh_ref[s - 1, c0:c0 + n, :] = ext_ref[c0 + s:c0 + s + n, :]

    def tap(r0, k):
        off = CONV_HALO - CONV_SIZE // 2 + k
        s = off % SUBLANES
        src = ext_ref if s == 0 else sh_ref.at[s - 1]
        return src[r0 + off - s:r0 + off - s + CONV_ROWS, :] * w_ref[k:k + 1, :]

    for c in range(tm // CONV_ROWS):
        r0 = c * CONV_ROWS
        acc = tap(r0, 0)
        for k in range(1, CONV_SIZE):
            acc = acc + tap(r0, k)
        co_ref[r0:r0 + CONV_ROWS, :] = acc + b_ref[...]
    hc = co_ref[...]
    mu = jnp.mean(hc, axis=-1, keepdims=True)
    var = jnp.mean(jnp.square(hc - mu), axis=-1, keepdims=True)
    ln = (hc - mu) * lax.rsqrt(var + NORM_EPS) * ng_ref[...] + nb_ref[...]
    o_ref[...] = (ln * jax.nn.sigmoid(ln)).astype(o_ref.dtype)


def _conv(proj, conv_w, conv_b, norm_g, norm_b, *, S, tm):
    T = proj.shape[0]
    ca, cg = COL_CONV_A // CONV_WIDTH, COL_CONV_G // CONV_WIDTH
    hb = tm // CONV_HALO
    last = T // CONV_HALO - 1

    def cur(c):
        return pl.BlockSpec((tm, CONV_WIDTH), lambda i: (i, c))

    def prev(c):
        return pl.BlockSpec((CONV_HALO, CONV_WIDTH), lambda i: (jnp.maximum(i * hb - 1, 0), c))

    def nxt(c):
        return pl.BlockSpec((CONV_HALO, CONV_WIDTH), lambda i: (jnp.minimum((i + 1) * hb, last), c))

    row = pl.BlockSpec((1, CONV_WIDTH), lambda i: (0, 0))
    return pl.pallas_call(
        functools.partial(_conv_body, tm=tm, S=S),
        grid=(T // tm,),
        in_specs=[cur(ca), prev(ca), nxt(ca), cur(cg), prev(cg), nxt(cg),
                  pl.BlockSpec((CONV_SIZE, CONV_WIDTH), lambda i: (0, 0)), row, row, row],
        out_specs=pl.BlockSpec((tm, CONV_WIDTH), lambda i: (i, 0)),
        out_shape=jax.ShapeDtypeStruct((T, CONV_WIDTH), BF16),
        scratch_shapes=[pltpu.VMEM((tm + 2 * CONV_HALO, CONV_WIDTH), F32),
                        pltpu.VMEM((SUBLANES - 1, tm + 2 * CONV_HALO - SUBLANES, CONV_WIDTH), F32),
                        pltpu.VMEM((tm, CONV_WIDTH), F32)],
        compiler_params=_params("parallel"),
        name="conv",
    )(proj, proj, proj, proj, proj, proj, conv_w, conv_b, norm_g, norm_b)


def _rotate(x, cos2, sin2):
    return x * cos2 + pltpu.roll(x, RET_QK_DIM // 2, 1) * sin2


def _ret_body(*refs, C, cpb, reverse):
    if reverse:
        (q_ref, k_ref, v_ref, cos_ref, sin_ref, dec_ref, xi_ref, zeta_ref, gch_ref,
         fwd_ref, gate_ref, norm_ref, o_ref, state_ref) = refs
    else:
        (q_ref, k_ref, v_ref, cos_ref, sin_ref, dec_ref, xi_ref, zeta_ref, gch_ref,
         o_ref, state_ref) = refs

    @pl.when(pl.program_id(1) == 0)
    def _():
        state_ref[...] = jnp.zeros_like(state_ref)

    order = range(cpb - 1, -1, -1) if reverse else range(cpb)
    for c in order:
        rows = slice(c * C, (c + 1) * C)
        cos2, sin2 = cos_ref[rows, :], sin_ref[rows, :]
        for h in range(RET_HEADS):
            qk = slice(h * RET_QK_DIM, (h + 1) * RET_QK_DIM)
            vs = slice(h * RET_V_DIM, (h + 1) * RET_V_DIM)
            qr = _rotate(q_ref[rows, qk].astype(F32), cos2, sin2)
            kr = _rotate(k_ref[rows, qk].astype(F32), cos2, sin2) * (RET_QK_DIM ** -0.5)
            vb = v_ref[rows, vs].astype(BF16)
            scores = lax.dot_general(qr.astype(BF16), kr.astype(BF16), (((1,), (1,)), ((), ())),
                                     preferred_element_type=F32) * dec_ref[h]
            inner = jnp.dot(scores.astype(BF16), vb, preferred_element_type=F32)
            state = state_ref[h]
            cross = jnp.dot((qr * xi_ref[h]).astype(BF16), state.astype(BF16),
                            preferred_element_type=F32)
            kv = lax.dot_general((kr * zeta_ref[h]).astype(BF16), vb, (((0,), (0,)), ((), ())),
                                 preferred_element_type=F32)
            state_ref[h] = state * gch_ref[h] + kv
            y = inner + cross
            if reverse:
                y = y + fwd_ref[rows, vs]
                y = y * lax.rsqrt(jnp.mean(y * y, axis=-1, keepdims=True) + NORM_EPS) * norm_ref[:, vs]
                gate = gate_ref[rows, vs].astype(F32)
                o_ref[rows, vs] = (y * (gate * jax.nn.sigmoid(gate))).astype(o_ref.dtype)
            else:
                o_ref[rows, vs] = y


def _ret_tables(log_gamma, C, reverse):
    idx = np.arange(C, dtype=np.float32)
    if reverse:
        diff = idx[None, :] - idx[:, None]
        tri = diff > 0
        q_pow, k_pow = C - idx, idx
    else:
        diff = idx[:, None] - idx[None, :]
        tri = diff >= 0
        q_pow, k_pow = idx + 1.0, C - 1.0 - idx
    lg = log_gamma[:, None, None]
    dec = jnp.where(tri[None], jnp.exp(np.where(tri, diff, 0.0)[None] * lg), 0.0)
    xi = jnp.broadcast_to(jnp.exp(q_pow[None, :, None] * lg), (RET_HEADS, C, RET_QK_DIM))
    zeta = jnp.broadcast_to(jnp.exp(k_pow[None, :, None] * lg), (RET_HEADS, C, RET_QK_DIM))
    gch = jnp.broadcast_to(jnp.exp(C * lg), (RET_HEADS, 1, RET_V_DIM))
    return dec, xi, zeta, gch


def _retention(proj, cos2, sin2, decay_logit, ret_norm, *, B, S, C, cpb):
    T = proj.shape[0]
    R = C * cpb
    NB = S // R
    log_gamma = jax.nn.log_sigmoid(decay_logit.astype(F32))
    cq, ck = COL_RET_Q // RET_QK_WIDTH, COL_RET_K // RET_QK_WIDTH
    cv, cg = COL_RET_V // RET_V_WIDTH, COL_RET_G // RET_V_WIDTH

    def run(reverse, extra_in, extra_specs, out_dtype):
        def blk(n):
            return NB - 1 - n if reverse else n

        tab = lambda shape: pl.BlockSpec(shape, lambda b, n: (0, 0, 0))
        in_specs = [
            pl.BlockSpec((R, RET_QK_WIDTH), lambda b, n: (b * NB + blk(n), cq)),
            pl.BlockSpec((R, RET_QK_WIDTH), lambda b, n: (b * NB + blk(n), ck)),
            pl.BlockSpec((R, RET_V_WIDTH), lambda b, n: (b * NB + blk(n), cv)),
            pl.BlockSpec((R, RET_QK_DIM), lambda b, n: (blk(n), 0)),
            pl.BlockSpec((R, RET_QK_DIM), lambda b, n: (blk(n), 0)),
            tab((RET_HEADS, C, C)), tab((RET_HEADS, C, RET_QK_DIM)), tab((RET_HEADS, C, RET_QK_DIM)),
            tab((RET_HEADS, 1, RET_V_DIM)),
        ] + extra_specs(blk)
        tables = _ret_tables(log_gamma[1 if reverse else 0], C, reverse)
        return pl.pallas_call(
            functools.partial(_ret_body, C=C, cpb=cpb, reverse=reverse),
            grid=(B, NB),
            in_specs=in_specs,
            out_specs=pl.BlockSpec((R, RET_V_WIDTH), lambda b, n: (b * NB + blk(n), 0)),
            out_shape=jax.ShapeDtypeStruct((T, RET_V_WIDTH), out_dtype),
            scratch_shapes=[pltpu.VMEM((RET_HEADS, RET_QK_DIM, RET_V_DIM), F32)],
            compiler_params=_params("parallel", "arbitrary"),
            name="ret_bwd" if reverse else "ret_fwd",
        )(proj, proj, proj, cos2, sin2, *tables, *extra_in)

    fwd = run(False, (), lambda blk: [], F32)
    return run(
        True, (fwd, proj, ret_norm),
        lambda blk: [pl.BlockSpec((R, RET_V_WIDTH), lambda b, n: (b * NB + blk(n), 0)),
                     pl.BlockSpec((R, RET_V_WIDTH), lambda b, n: (b * NB + blk(n), cg)),
                     pl.BlockSpec((1, RET_V_WIDTH), lambda b, n: (0, 0))],
        BF16)


ATTN_SUB = 128
LSE_LANES = HEAD_DIM // ATTN_HEADS_PER_GROUP


def _attn_body(q_ref, kc_ref, kp_ref, kn_ref, vc_ref, vp_ref, vn_ref, bias_ref, qg_ref, kg_ref,
               o_ref, l_ref, *, tq, L):
    n = pl.program_id(2)
    nk = ATTN_SUB + 2 * ATTN_RADIUS
    row = lax.broadcasted_iota(jnp.int32, (ATTN_SUB, nk), 0)
    col = lax.broadcasted_iota(jnp.int32, (ATTN_SUB, nk), 1)
    band = jnp.abs(col - ATTN_RADIUS - row) <= ATTN_RADIUS
    lane_head = lax.broadcasted_iota(jnp.int32, (ATTN_SUB, HEAD_DIM), 1) // LSE_LANES
    for h in range(ATTN_HEADS_PER_GROUP):
        cs = slice(h * HEAD_DIM, (h + 1) * HEAD_DIM)
        q = (_rms(q_ref[:, cs].astype(F32), qg_ref[...]) * (HEAD_DIM ** -0.5)).astype(BF16)
        kext = jnp.concatenate([kp_ref[:, cs], kc_ref[:, cs], kn_ref[:, cs]], axis=0)
        kext = _rms(kext.astype(F32), kg_ref[...]).astype(BF16)
        vext = jnp.concatenate([vp_ref[:, cs], vc_ref[:, cs], vn_ref[:, cs]], axis=0)
        for s in range(tq // ATTN_SUB):
            r0 = s * ATTN_SUB
            kpos = n * tq + r0 - ATTN_RADIUS + col
            mask = band & (kpos >= 0) & (kpos < L)
            logits = lax.dot_general(q[r0:r0 + ATTN_SUB], kext[r0:r0 + nk], (((1,), (1,)), ((), ())),
                                     preferred_element_type=F32) + bias_ref[h]
            logits = jnp.where(mask, logits, NEG_INF)
            m = jnp.max(logits, axis=-1, keepdims=True)
            e = jnp.exp(logits - m)
            ssum = jnp.sum(e, axis=-1, keepdims=True)
            o = jnp.dot(e.astype(BF16), vext[r0:r0 + nk], preferred_element_type=F32) / ssum
            o_ref[r0:r0 + ATTN_SUB, cs] = o.astype(o_ref.dtype)
            lse = jnp.broadcast_to(m + jnp.log(ssum), (ATTN_SUB, HEAD_DIM))
            if h > 0:
                lse = jnp.where(lane_head == h, lse, l_ref[r0:r0 + ATTN_SUB, :])
            l_ref[r0:r0 + ATTN_SUB, :] = lse


def _attn_group(qkv, cols, bias, q_norm, k_norm, *, g, tq):
    B, dil, L, _ = qkv.shape
    tq = min(tq, L)
    gw = ATTN_GROUP_WIDTH
    cq, ck, cv = cols
    hb = tq // ATTN_RADIUS
    last = L // ATTN_RADIUS - 1

    def cur(c):
        return pl.BlockSpec((None, None, tq, gw), lambda b, r, n: (b, r, n, c))

    def prev(c):
        return pl.BlockSpec((None, None, ATTN_RADIUS, gw),
                            lambda b, r, n: (b, r, jnp.maximum(n * hb - 1, 0), c))

    def nxt(c):
        return pl.BlockSpec((None, None, ATTN_RADIUS, gw),
                            lambda b, r, n: (b, r, jnp.minimum((n + 1) * hb, last), c))

    gain = pl.BlockSpec((1, HEAD_DIM), lambda b, r, n: (0, 0))
    out_spec = pl.BlockSpec((None, None, tq, gw), lambda b, r, n: (b, r, n, 0))
    out_shape = jax.ShapeDtypeStruct((B, dil, L, gw), F32)
    return pl.pallas_call(
        functools.partial(_attn_body, tq=tq, L=L),
        grid=(B, dil, L // tq),
        in_specs=[cur(cq), cur(ck), prev(ck), nxt(ck), cur(cv), prev(cv), nxt(cv),
                  pl.BlockSpec(bias.shape, lambda b, r, n: (0, 0, 0)), gain, gain],
        out_specs=[out_spec, pl.BlockSpec((None, None, tq, HEAD_DIM), lambda b, r, n: (b, r, n, 0))],
        out_shape=[jax.ShapeDtypeStruct(out_shape.shape, BF16),
                   jax.ShapeDtypeStruct((B, dil, L, HEAD_DIM), F32)],
        compiler_params=_params("parallel", "parallel", "arbitrary"),
        name=f"attn_g{g}",
    )(qkv, qkv, qkv, qkv, qkv, qkv, qkv, bias, q_norm, k_norm)


def _t5_bucket(rel):
    half = REL_BUCKETS // 2
    exact = half // 2
    offset = np.where(rel > 0, half, 0)
    n = np.abs(rel)
    large = exact + (np.log(np.maximum(n, 1) / exact) / np.log(REL_MAX_DISTANCE / exact)
                     * (half - exact)).astype(np.int32)
    large = np.minimum(large, half - 1)
    return (offset + np.where(n < exact, n, large)).astype(np.int32)


def _attn_bias(rel_bias, g):
    dil = ATTN_PATTERNS[g][1]
    nk = ATTN_SUB + 2 * ATTN_RADIUS
    span = nk - ATTN_RADIUS - 1
    period = 2 * span + 2
    rel = np.arange(period) - span
    table = rel_bias[:, g * ATTN_HEADS_PER_GROUP:(g + 1) * ATTN_HEADS_PER_GROUP]
    v = jnp.take(table, _t5_bucket(rel * dil), axis=0).T.astype(F32)
    skew = jnp.tile(v, (1, ATTN_SUB))[:, :ATTN_SUB * (period - 1)].reshape(-1, ATTN_SUB, period - 1)
    return skew[:, :, ATTN_SUB - 1:ATTN_SUB - 1 + nk]


def _attn_merge_body(o0_ref, o1_ref, o2_ref, l0_ref, l1_ref, l2_ref, y_ref):
    o_refs = (o0_ref, o1_ref, o2_ref)
    l_refs = (l0_ref, l1_ref, l2_ref)
    for hs in range(ATTN_HEADS_PER_GROUP):
        ls = [r[:, hs * LSE_LANES:hs * LSE_LANES + 1] for r in l_refs]
        m = jnp.maximum(jnp.maximum(ls[0], ls[1]), ls[2])
        ws = [jnp.exp(l - m) for l in ls]
        tot = ws[0] + ws[1] + ws[2]
        for g in range(len(ATTN_PATTERNS)):
            o = o_refs[g][:, hs * HEAD_DIM:(hs + 1) * HEAD_DIM].astype(F32)
            c0 = g * ATTN_GROUP_WIDTH + hs * HEAD_DIM
            y_ref[:, c0:c0 + HEAD_DIM] = (o * (ws[g] / tot)).astype(y_ref.dtype)


def _attn_merge(outs, lses, *, tm):
    T = outs[0].shape[0]
    gw = ATTN_GROUP_WIDTH
    specs = [pl.BlockSpec((tm, gw), lambda i: (i, 0)) for _ in ATTN_PATTERNS]
    return pl.pallas_call(
        _attn_merge_body,
        grid=(T // tm,),
        in_specs=specs + [pl.BlockSpec((tm, HEAD_DIM), lambda i: (i, 0)) for _ in ATTN_PATTERNS],
        out_specs=pl.BlockSpec((tm, ATTN_WIDTH), lambda i: (i, 0)),
        out_shape=jax.ShapeDtypeStruct((T, ATTN_WIDTH), BF16),
        compiler_params=_params("parallel"),
        name="attn_merge",
    )(*outs, *lses)


def _attention(proj, rel_bias, q_norm, k_norm, *, B, S, tq, tm):
    T = proj.shape[0]
    gw = ATTN_GROUP_WIDTH
    outs, lses = [], []
    for g, (_, dil) in enumerate(ATTN_PATTERNS):
        L = S // dil
        start = COL_ATT + g * 3 * gw
        if dil == 1:
            qkv, cols = proj.reshape(B, 1, S, IN_WIDTH), [start // gw + c for c in range(3)]
        else:
            qkv = proj[:, start:start + 3 * gw].reshape(B, L, dil, 3 * gw).transpose(0, 2, 1, 3)
            cols = [0, 1, 2]
        o, l = _attn_group(qkv, cols, _attn_bias(rel_bias, g), q_norm, k_norm, g=g, tq=tq)
        outs.append(o.transpose(0, 2, 1, 3).reshape(T, gw))
        lses.append(l.transpose(0, 2, 1, 3).reshape(T, HEAD_DIM))
    return _attn_merge(outs, lses, tm=tm)


def _merge_body(h_ref, g_ref, yp_ref, ya_ref, yr_ref, yc_ref,
                wg0_ref, wg1_ref, wg2_ref, wg3_ref, bg0_ref, bg1_ref, bg2_ref, bg3_ref,
                wp_ref, wa_ref, wr_ref, wc_ref, wo_ref, o_ref, u_ref):
    j = pl.program_id(1)

    @pl.when(j == 0)
    def _():
        u_ref[...] = _rms(h_ref[...], g_ref[...]).astype(BF16)
        o_ref[...] = h_ref[...]

    u = u_ref[...]
    merged = None
    for y_ref, wb_ref, wg_ref, bg_ref in ((yp_ref, wp_ref, wg0_ref, bg0_ref),
                                          (ya_ref, wa_ref, wg1_ref, bg1_ref),
                                          (yr_ref, wr_ref, wg2_ref, bg2_ref),
                                          (yc_ref, wc_ref, wg3_ref, bg3_ref)):
        gate = jax.nn.sigmoid(jnp.dot(u, wg_ref[...], preferred_element_type=F32) + bg_ref[...])
        term = gate * jnp.dot(y_ref[...], wb_ref[...], preferred_element_type=F32)
        merged = term if merged is None else merged + term
    o_ref[...] += jnp.dot(merged.astype(BF16), wo_ref[...], preferred_element_type=F32)


def _merge(h, gain, ys, w_gate, b_gate, w_brs, w_out, *, tm, tn):
    T, D = h.shape
    nb = D // tn
    row = lambda width: pl.BlockSpec((tm, width), lambda i, j: (i, 0))
    gate_specs = [pl.BlockSpec((D, tn), functools.partial(lambda i, j, b: (0, b * nb + j), b=b))
                  for b in range(N_BRANCHES)]
    bias_specs = [pl.BlockSpec((1, tn), functools.partial(lambda i, j, b: (0, b * nb + j), b=b))
                  for b in range(N_BRANCHES)]
    br_specs = [pl.BlockSpec((w.shape[0], tn), lambda i, j: (0, j)) for w in w_brs]
    return pl.pallas_call(
        _merge_body,
        grid=(T // tm, nb),
        in_specs=[row(D), pl.BlockSpec((1, D), lambda i, j: (0, 0))]
        + [row(y.shape[1]) for y in ys] + gate_specs + bias_specs + br_specs
        + [pl.BlockSpec((tn, D), lambda i, j: (j, 0))],
        out_specs=row(D),
        out_shape=jax.ShapeDtypeStruct((T, D), F32),
        scratch_shapes=[pltpu.VMEM((tm, D), BF16)],
        compiler_params=_params("parallel", "arbitrary"),
        name="merge",
    )(h, gain, *ys, *([w_gate] * N_BRANCHES), *([b_gate] * N_BRANCHES), *w_brs, w_out)


def _ple_body(h_ref, g_ref, p_ref, wg_ref, wp_ref, o_ref):
    h = h_ref[...]
    gate = jax.nn.sigmoid(jnp.dot(_rms(h, g_ref[...]).astype(BF16), wg_ref[...],
                                  preferred_element_type=F32))
    ple = jnp.dot(p_ref[...].astype(BF16), wp_ref[...], preferred_element_type=F32)
    o_ref[...] = h + gate * ple


def _ple(h, gain, p, w_gate, w_proj, *, tm):
    T, D = h.shape
    P = p.shape[1]
    return pl.pallas_call(
        _ple_body,
        grid=(T // tm,),
        in_specs=[
            pl.BlockSpec((tm, D), lambda i: (i, 0)),
            pl.BlockSpec((1, D), lambda i: (0, 0)),
            pl.BlockSpec((tm, P), lambda i: (i, 0)),
            pl.BlockSpec((D, D), lambda i: (0, 0)),
            pl.BlockSpec((P, D), lambda i: (0, 0)),
        ],
        out_specs=pl.BlockSpec((tm, D), lambda i: (i, 0)),
        out_shape=jax.ShapeDtypeStruct((T, D), F32),
        compiler_params=_params("parallel"),
        name="ple",
    )(h, gain, p, w_gate, w_proj)


def _permute_w_in(w_in):
    xp, aq, ak, av, rq, rk, rv, rg, cin = jnp.split(
        w_in, np.cumsum((1024, 1536, 1536, 1536, 512, 512, 1024, 1024))[:], axis=-1)
    gw = ATTN_GROUP_WIDTH
    att = [w[:, g * gw:(g + 1) * gw] for g in range(len(ATTN_PATTERNS)) for w in (aq, ak, av)]
    return jnp.concatenate([cin, xp, rv, rg] + att + [rq, rk], axis=-1)


def _rope_tables(S):
    half = RET_QK_DIM // 2
    pos = jnp.arange(S, dtype=F32)
    inv = ROPE_BASE ** (-jnp.linspace(0.0, 1.0, half, dtype=F32))
    ang = pos[:, None] * inv[None, :]
    cos, sin = jnp.cos(ang), jnp.sin(ang)
    return jnp.concatenate([cos, cos], axis=-1), jnp.concatenate([-sin, sin], axis=-1)


def kernel(x, p, rel_bias, ffn1_norm, ffn1_w_gate, ffn1_w_up, ffn1_w_down, mix_norm, w_in,
           pool_w, pool_scale, q_norm, k_norm, ret_decay_logit, ret_norm, conv_w, conv_b,
           conv_norm_g, conv_norm_b, w_gate, b_gate, w_br_pool, w_br_attn, w_br_ret, w_br_conv,
           w_out, ffn2_norm, ffn2_w_gate, ffn2_w_up, ffn2_w_down, ple_norm, w_ple_gate, w_ple_proj):
    B, S, D = x.shape
    T = B * S
    depth = p.shape[0]
    cos2, sin2 = _rope_tables(S)
    h = x.reshape(T, D)
    row = lambda v: v.reshape(1, -1)
    bf = lambda w: w.astype(BF16)
    for i in range(depth):
        h = _ffn(h, row(ffn1_norm[i]), bf(ffn1_w_gate[i]), bf(ffn1_w_up[i]), bf(ffn1_w_down[i]),
                 tm=1024, tf=512)
        proj = _inproj(h, row(mix_norm[i]), bf(_permute_w_in(w_in[i])), tm=1024, tn=1536)
        y_pool = _pool(proj, bf(pool_w[i]), row(pool_scale[i]), S=S, tm=512)
        y_attn = _attention(proj, rel_bias, row(q_norm[i]), row(k_norm[i]), B=B, S=S, tq=512, tm=512)
        y_ret = _retention(proj, cos2, sin2, ret_decay_logit[i], row(ret_norm[i]), B=B, S=S, C=128, cpb=4)
        y_conv = _conv(proj, conv_w[i], row(conv_b[i]), row(conv_norm_g[i]), row(conv_norm_b[i]),
                       S=S, tm=256)
        h = _merge(h, row(mix_norm[i]), (y_pool, y_attn, y_ret, y_conv), bf(w_gate[i]), row(b_gate[i]),
                   (bf(w_br_pool[i]), bf(w_br_attn[i]), bf(w_br_ret[i]), bf(w_br_conv[i])),
                   bf(w_out[i]), tm=512, tn=256)
        h = _ffn(h, row(ffn2_norm[i]), bf(ffn2_w_gate[i]), bf(ffn2_w_up[i]), bf(ffn2_w_down[i]),
                 tm=1024, tf=512)
        h = _ple(h, row(ple_norm[i]), p[i].reshape(T, -1), bf(w_ple_gate[i]), bf(w_ple_proj[i]), tm=512)
    return h.reshape(B, S, D)
```

```python
import functools

import numpy as np
import jax
import jax.numpy as jnp
from jax import lax
from jax.experimental import pallas as pl
from jax.experimental.pallas import tpu as pltpu

F32 = jnp.float32
BF16 = jnp.bfloat16

D_MODEL = 2048
HEAD_DIM = 128
NORM_EPS = 1e-6
NEG_INF = -1e30
POOL_WINDOWS = (2, 4, 8, 16)
POOL_WIDTH = 1024
POOL_GROUP_WIDTH = 256
POOL_HALO = 16
ATTN_PATTERNS = ((128, 1), (512, 4), (2048, 16))
ATTN_HEADS_PER_GROUP = 4
ATTN_HEADS = 12
ATTN_WIDTH = ATTN_HEADS * HEAD_DIM
ATTN_GROUP_WIDTH = ATTN_HEADS_PER_GROUP * HEAD_DIM
ATTN_RADIUS = 64
REL_BUCKETS = 32
REL_MAX_DISTANCE = 1024
RET_HEADS = 4
RET_QK_DIM = 128
RET_V_DIM = 256
RET_QK_WIDTH = RET_HEADS * RET_QK_DIM
RET_V_WIDTH = RET_HEADS * RET_V_DIM
ROPE_BASE = 10000.0
CONV_WIDTH = 1024
CONV_SIZE = 31
CONV_HALO = 16
N_BRANCHES = 4

IN_WIDTH = 10752
COL_CONV_A = 0
COL_CONV_G = 1024
COL_POOL = 2048
COL_RET_V = 3072
COL_RET_G = 4096
COL_ATT = 5120
COL_RET_Q = 9728
COL_RET_K = 10240

VMEM_LIMIT = 60 * 1024 * 1024


def _params(*sem):
    return pltpu.CompilerParams(dimension_semantics=sem, vmem_limit_bytes=VMEM_LIMIT)


def _rms(x, gain):
    return x * lax.rsqrt(jnp.mean(x * x, axis=-1, keepdims=True) + NORM_EPS) * gain


def _ffn_body(h_ref, g_ref, wg_ref, wu_ref, wd_ref, o_ref, xn_ref):
    j = pl.program_id(1)

    @pl.when(j == 0)
    def _():
        xn_ref[...] = _rms(h_ref[...], g_ref[...]).astype(BF16)
        o_ref[...] = jnp.zeros_like(o_ref)

    xn = xn_ref[...]
    a = jnp.dot(xn, wg_ref[...], preferred_element_type=F32)
    b = jnp.dot(xn, wu_ref[...], preferred_element_type=F32)
    mid = (a * jax.nn.sigmoid(a) * b).astype(BF16)
    o_ref[...] += jnp.dot(mid, wd_ref[...], preferred_element_type=F32)

    @pl.when(j == pl.num_programs(1) - 1)
    def _():
        o_ref[...] = h_ref[...] + 0.5 * o_ref[...]


def _ffn(h, gain, wg, wu, wd, *, tm, tf):
    T, D = h.shape
    Fd = wg.shape[1]
    return pl.pallas_call(
        _ffn_body,
        grid=(T // tm, Fd // tf),
        in_specs=[
            pl.BlockSpec((tm, D), lambda i, j: (i, 0)),
            pl.BlockSpec((1, D), lambda i, j: (0, 0)),
            pl.BlockSpec((D, tf), lambda i, j: (0, j)),
            pl.BlockSpec((D, tf), lambda i, j: (0, j)),
            pl.BlockSpec((tf, D), lambda i, j: (j, 0)),
        ],
        out_specs=pl.BlockSpec((tm, D), lambda i, j: (i, 0)),
        out_shape=jax.ShapeDtypeStruct((T, D), F32),
        scratch_shapes=[pltpu.VMEM((tm, D), BF16)],
        compiler_params=_params("parallel", "arbitrary"),
        name="ffn",
    )(h, gain, wg, wu, wd)


def _inproj_body(h_ref, g_ref, w_ref, o_ref, xn_ref):
    @pl.when(pl.program_id(1) == 0)
    def _():
        xn_ref[...] = _rms(h_ref[...], g_ref[...]).astype(BF16)

    o_ref[...] = jnp.dot(xn_ref[...], w_ref[...], preferred_element_type=F32).astype(o_ref.dtype)


def _inproj(h, gain, w, *, tm, tn):
    T, D = h.shape
    N = w.shape[1]
    return pl.pallas_call(
        _inproj_body,
        grid=(T // tm, N // tn),
        in_specs=[
            pl.BlockSpec((tm, D), lambda i, j: (i, 0)),
            pl.BlockSpec((1, D), lambda i, j: (0, 0)),
            pl.BlockSpec((D, tn), lambda i, j: (0, j)),
        ],
        out_specs=pl.BlockSpec((tm, tn), lambda i, j: (i, j)),
        out_shape=jax.ShapeDtypeStruct((T, N), BF16),
        scratch_shapes=[pltpu.VMEM((tm, D), BF16)],
        compiler_params=_params("parallel", "arbitrary"),
        name="inproj",
    )(h, gain, w)


def _pool_body(cur_ref, prev_ref, next_ref, w_ref, sc_ref, o_ref, ext_ref, *, tm, S):
    pos0 = (pl.program_id(0) * tm) % S
    ext_ref[0:POOL_HALO, :] = jnp.where(pos0 == 0, 0.0, prev_ref[...].astype(F32))
    ext_ref[POOL_HALO:POOL_HALO + tm, :] = cur_ref[...].astype(F32)
    ext_ref[POOL_HALO + tm:, :] = jnp.where(pos0 + tm == S, 0.0, next_ref[...].astype(F32))
    t = pos0 + lax.broadcasted_iota(jnp.int32, (tm, 1), 0)
    for g, w in enumerate(POOL_WINDOWS):
        cs = slice(g * POOL_GROUP_WIDTH, (g + 1) * POOL_GROUP_WIDTH)
        tot = ext_ref[POOL_HALO - w // 2:POOL_HALO - w // 2 + tm, cs]
        for k in range(-w // 2 + 1, w // 2):
            tot = tot + ext_ref[POOL_HALO + k:POOL_HALO + k + tm, cs]
        cnt = jnp.minimum(t + w // 2, S) - jnp.maximum(t - w // 2, 0)
        mixed = tot / cnt.astype(F32) - ext_ref[POOL_HALO:POOL_HALO + tm, cs]
        y = jnp.dot(mixed.astype(BF16), w_ref[g], preferred_element_type=F32)
        o_ref[:, cs] = (y * sc_ref[:, cs]).astype(o_ref.dtype)


def _pool(proj, pool_w, pool_scale, *, S, tm):
    T = proj.shape[0]
    cb = COL_POOL // POOL_WIDTH
    hb = tm // POOL_HALO
    last = T // POOL_HALO - 1
    return pl.pallas_call(
        functools.partial(_pool_body, tm=tm, S=S),
        grid=(T // tm,),
        in_specs=[
            pl.BlockSpec((tm, POOL_WIDTH), lambda i: (i, cb)),
            pl.BlockSpec((POOL_HALO, POOL_WIDTH), lambda i: (jnp.maximum(i * hb - 1, 0), cb)),
            pl.BlockSpec((POOL_HALO, POOL_WIDTH), lambda i: (jnp.minimum((i + 1) * hb, last), cb)),
            pl.BlockSpec(pool_w.shape, lambda i: (0, 0, 0)),
            pl.BlockSpec((1, POOL_WIDTH), lambda i: (0, 0)),
        ],
        out_specs=pl.BlockSpec((tm, POOL_WIDTH), lambda i: (i, 0)),
        out_shape=jax.ShapeDtypeStruct((T, POOL_WIDTH), BF16),
        scratch_shapes=[pltpu.VMEM((tm + 2 * POOL_HALO, POOL_WIDTH), F32)],
        compiler_params=_params("parallel"),
        name="pool",
    )(proj, proj, proj, pool_w, pool_scale)


CONV_ROWS = 32
CONV_SHIFT_ROWS = 56
SUBLANES = 8


def _conv_body(a_ref, ap_ref, an_ref, g_ref, gp_ref, gn_ref, w_ref, b_ref, ng_ref, nb_ref,
               o_ref, ext_ref, sh_ref, co_ref, *, tm, S):
    pos0 = (pl.program_id(0) * tm) % S

    def glu(x_ref, gate_ref):
        return x_ref[...].astype(F32) * jax.nn.sigmoid(gate_ref[...].astype(F32))

    ext_ref[0:CONV_HALO, :] = jnp.where(pos0 == 0, 0.0, glu(ap_ref, gp_ref))
    ext_ref[CONV_HALO:CONV_HALO + tm, :] = glu(a_ref, g_ref)
    ext_ref[CONV_HALO + tm:, :] = jnp.where(pos0 + tm == S, 0.0, glu(an_ref, gn_ref))
    sh_rows = tm + 2 * CONV_HALO - SUBLANES
    for s in range(1, SUBLANES):
        for c0 in range(0, sh_rows, CONV_SHIFT_ROWS):
            n = min(CONV_SHIFT_ROWS, sh_rows - c0)
            sh_ref[s - 1, c0:c0 + n, :] = ext_ref[c0 + s:c0 + s + n, :]

    def tap(r0, k):
        off = CONV_HALO - CONV_SIZE // 2 + k
        s = off % SUBLANES
        src = ext_ref if s == 0 else sh_ref.at[s - 1]
        return src[r0 + off - s:r0 + off - s + CONV_ROWS, :] * w_ref[k:k + 1, :]

    for c in range(tm // CONV_ROWS):
        r0 = c * CONV_ROWS
        acc = tap(r0, 0)
        for k in range(1, CONV_SIZE):
            acc = acc + tap(r0, k)
        co_ref[r0:r0 + CONV_ROWS, :] = acc + b_ref[...]
    hc = co_ref[...]
    mu = jnp.mean(hc, axis=-1, keepdims=True)
    var = jnp.mean(jnp.square(hc - mu), axis=-1, keepdims=True)
    ln = (hc - mu) * lax.rsqrt(var + NORM_EPS) * ng_ref[...] + nb_ref[...]
    o_ref[...] = (ln * jax.nn.sigmoid(ln)).astype(o_ref.dtype)


def _conv(proj, conv_w, conv_b, norm_g, norm_b, *, S, tm):
    T = proj.shape[0]
    ca, cg = COL_CONV_A // CONV_WIDTH, COL_CONV_G // CONV_WIDTH
    hb = tm // CONV_HALO
    last = T // CONV_HALO - 1

    def cur(c):
        return pl.BlockSpec((tm, CONV_WIDTH), lambda i: (i, c))

    def prev(c):
        return pl.BlockSpec((CONV_HALO, CONV_WIDTH), lambda i: (jnp.maximum(i * hb - 1, 0), c))

    def nxt(c):
        return pl.BlockSpec((CONV_HALO, CONV_WIDTH), lambda i: (jnp.minimum((i + 1) * hb, last), c))

    row = pl.BlockSpec((1, CONV_WIDTH), lambda i: (0, 0))
    return pl.pallas_call(
        functools.partial(_conv_body, tm=tm, S=S),
        grid=(T // tm,),
        in_specs=[cur(ca), prev(ca), nxt(ca), cur(cg), prev(cg), nxt(cg),
                  pl.BlockSpec((CONV_SIZE, CONV_WIDTH), lambda i: (0, 0)), row, row, row],
        out_specs=pl.BlockSpec((tm, CONV_WIDTH), lambda i: (i, 0)),
        out_shape=jax.ShapeDtypeStruct((T, CONV_WIDTH), BF16),
        scratch_shapes=[pltpu.VMEM((tm + 2 * CONV_HALO, CONV_WIDTH), F32),
                        pltpu.VMEM((SUBLANES - 1, tm + 2 * CONV_HALO - SUBLANES, CONV_WIDTH), F32),
                        pltpu.VMEM((tm, CONV_WIDTH), F32)],
        compiler_params=_params("parallel"),
        name="conv",
    )(proj, proj, proj, proj, proj, proj, conv_w, conv_b, norm_g, norm_b)


def _rotate(x, cos2, sin2):
    return x * cos2 + pltpu.roll(x, RET_QK_DIM // 2, 1) * sin2


def _ret_body(*refs, C, cpb, reverse):
    if reverse:
        (q_ref, k_ref, v_ref, cos_ref, sin_ref, dec_ref, xi_ref, zeta_ref, gch_ref,
         fwd_ref, gate_ref, norm_ref, o_ref, state_ref) = refs
    else:
        (q_ref, k_ref, v_ref, cos_ref, sin_ref, dec_ref, xi_ref, zeta_ref, gch_ref,
         o_ref, state_ref) = refs

    @pl.when(pl.program_id(1) == 0)
    def _():
        state_ref[...] = jnp.zeros_like(state_ref)

    order = range(cpb - 1, -1, -1) if reverse else range(cpb)
    for c in order:
        rows = slice(c * C, (c + 1) * C)
        cos2, sin2 = cos_ref[rows, :], sin_ref[rows, :]
        for h in range(RET_HEADS):
            qk = slice(h * RET_QK_DIM, (h + 1) * RET_QK_DIM)
            vs = slice(h * RET_V_DIM, (h + 1) * RET_V_DIM)
            qr = _rotate(q_ref[rows, qk].astype(F32), cos2, sin2)
            kr = _rotate(k_ref[rows, qk].astype(F32), cos2, sin2) * (RET_QK_DIM ** -0.5)
            vb = v_ref[rows, vs].astype(BF16)
            scores = lax.dot_general(qr.astype(BF16), kr.astype(BF16), (((1,), (1,)), ((), ())),
                                     preferred_element_type=F32) * dec_ref[h]
            inner = jnp.dot(scores.astype(BF16), vb, preferred_element_type=F32)
            state = state_ref[h]
            cross = jnp.dot((qr * xi_ref[h]).astype(BF16), state.astype(BF16),
                            preferred_element_type=F32)
            kv = lax.dot_general((kr * zeta_ref[h]).astype(BF16), vb, (((0,), (0,)), ((), ())),
                                 preferred_element_type=F32)
            state_ref[h] = state * gch_ref[h] + kv
            y = inner + cross
            if reverse:
                y = y + fwd_ref[rows, vs]
                y = y * lax.rsqrt(jnp.mean(y * y, axis=-1, keepdims=True) + NORM_EPS) * norm_ref[:, vs]
                gate = gate_ref[rows, vs].astype(F32)
                o_ref[rows, vs] = (y * (gate * jax.nn.sigmoid(gate))).astype(o_ref.dtype)
            else:
                o_ref[rows, vs] = y


def _ret_tables(log_gamma, C, reverse):
    idx = np.arange(C, dtype=np.float32)
    if reverse:
        diff = idx[None, :] - idx[:, None]
        tri = diff > 0
        q_pow, k_pow = C - idx, idx
    else:
        diff = idx[:, None] - idx[None, :]
        tri = diff >= 0
        q_pow, k_pow = idx + 1.0, C - 1.0 - idx
    lg = log_gamma[:, None, None]
    dec = jnp.where(tri[None], jnp.exp(np.where(tri, diff, 0.0)[None] * lg), 0.0)
    xi = jnp.broadcast_to(jnp.exp(q_pow[None, :, None] * lg), (RET_HEADS, C, RET_QK_DIM))
    zeta = jnp.broadcast_to(jnp.exp(k_pow[None, :, None] * lg), (RET_HEADS, C, RET_QK_DIM))
    gch = jnp.broadcast_to(jnp.exp(C * lg), (RET_HEADS, 1, RET_V_DIM))
    return dec, xi, zeta, gch


def _retention(proj, cos2, sin2, decay_logit, ret_norm, *, B, S, C, cpb):
    T = proj.shape[0]
    R = C * cpb
    NB = S // R
    log_gamma = jax.nn.log_sigmoid(decay_logit.astype(F32))
    cq, ck = COL_RET_Q // RET_QK_WIDTH, COL_RET_K // RET_QK_WIDTH
    cv, cg = COL_RET_V // RET_V_WIDTH, COL_RET_G // RET_V_WIDTH

    def run(reverse, extra_in, extra_specs, out_dtype):
        def blk(n):
            return NB - 1 - n if reverse else n

        tab = lambda shape: pl.BlockSpec(shape, lambda b, n: (0, 0, 0))
        in_specs = [
            pl.BlockSpec((R, RET_QK_WIDTH), lambda b, n: (b * NB + blk(n), cq)),
            pl.BlockSpec((R, RET_QK_WIDTH), lambda b, n: (b * NB + blk(n), ck)),
            pl.BlockSpec((R, RET_V_WIDTH), lambda b, n: (b * NB + blk(n), cv)),
            pl.BlockSpec((R, RET_QK_DIM), lambda b, n: (blk(n), 0)),
            pl.BlockSpec((R, RET_QK_DIM), lambda b, n: (blk(n), 0)),
            tab((RET_HEADS, C, C)), tab((RET_HEADS, C, RET_QK_DIM)), tab((RET_HEADS, C, RET_QK_DIM)),
            tab((RET_HEADS, 1, RET_V_DIM)),
        ] + extra_specs(blk)
        tables = _ret_tables(log_gamma[1 if reverse else 0], C, reverse)
        return pl.pallas_call(
            functools.partial(_ret_body, C=C, cpb=cpb, reverse=reverse),
            grid=(B, NB),
            in_specs=in_specs,
            out_specs=pl.BlockSpec((R, RET_V_WIDTH), lambda b, n: (b * NB + blk(n), 0)),
            out_shape=jax.ShapeDtypeStruct((T, RET_V_WIDTH), out_dtype),
            scratch_shapes=[pltpu.VMEM((RET_HEADS, RET_QK_DIM, RET_V_DIM), F32)],
            compiler_params=_params("parallel", "arbitrary"),
            name="ret_bwd" if reverse else "ret_fwd",
        )(proj, proj, proj, cos2, sin2, *tables, *extra_in)

    fwd = run(False, (), lambda blk: [], F32)
    return run(
        True, (fwd, proj, ret_norm),
        lambda blk: [pl.BlockSpec((R, RET_V_WIDTH), lambda b, n: (b * NB + blk(n), 0)),
                     pl.BlockSpec((R, RET_V_WIDTH), lambda b, n: (b * NB + blk(n), cg)),
                     pl.BlockSpec((1, RET_V_WIDTH), lambda b, n: (0, 0))],
        BF16)


ATTN_SUB = 128
LSE_LANES = HEAD_DIM // ATTN_HEADS_PER_GROUP


def _attn_body(q_ref, kc_ref, kp_ref, kn_ref, vc_ref, vp_ref, vn_ref, bias_ref, qg_ref, kg_ref,
               o_ref, l_ref, *, tq, L):
    n = pl.program_id(2)
    nk = ATTN_SUB + 2 * ATTN_RADIUS
    row = lax.broadcasted_iota(jnp.int32, (ATTN_SUB, nk), 0)
    col = lax.broadcasted_iota(jnp.int32, (ATTN_SUB, nk), 1)
    band = jnp.abs(col - ATTN_RADIUS - row) <= ATTN_RADIUS
    lane_head = lax.broadcasted_iota(jnp.int32, (ATTN_SUB, HEAD_DIM), 1) // LSE_LANES
    for h in range(ATTN_HEADS_PER_GROUP):
        cs = slice(h * HEAD_DIM, (h + 1) * HEAD_DIM)
        q = (_rms(q_ref[:, cs].astype(F32), qg_ref[...]) * (HEAD_DIM ** -0.5)).astype(BF16)
        kext = jnp.concatenate([kp_ref[:, cs], kc_ref[:, cs], kn_ref[:, cs]], axis=0)
        kext = _rms(kext.astype(F32), kg_ref[...]).astype(BF16)
        vext = jnp.concatenate([vp_ref[:, cs], vc_ref[:, cs], vn_ref[:, cs]], axis=0)
        for s in range(tq // ATTN_SUB):
            r0 = s * ATTN_SUB
            kpos = n * tq + r0 - ATTN_RADIUS + col
            mask = band & (kpos >= 0) & (kpos < L)
            logits = lax.dot_general(q[r0:r0 + ATTN_SUB], kext[r0:r0 + nk], (((1,), (1,)), ((), ())),
                                     preferred_element_type=F32) + bias_ref[h]
            logits = jnp.where(mask, logits, NEG_INF)
            m = jnp.max(logits, axis=-1, keepdims=True)
            e = jnp.exp(logits - m)
            ssum = jnp.sum(e, axis=-1, keepdims=True)
            o = jnp.dot(e.astype(BF16), vext[r0:r0 + nk], preferred_element_type=F32) / ssum
            o_ref[r0:r0 + ATTN_SUB, cs] = o.astype(o_ref.dtype)
            lse = jnp.broadcast_to(m + jnp.log(ssum), (ATTN_SUB, HEAD_DIM))
            if h > 0:
                lse = jnp.where(lane_head == h, lse, l_ref[r0:r0 + ATTN_SUB, :])
            l_ref[r0:r0 + ATTN_SUB, :] = lse


def _attn_group(qkv, cols, bias, q_norm, k_norm, *, g, tq):
    B, dil, L, _ = qkv.shape
    tq = min(tq, L)
    gw = ATTN_GROUP_WIDTH
    cq, ck, cv = cols
    hb = tq // ATTN_RADIUS
    last = L // ATTN_RADIUS - 1

    def cur(c):
        return pl.BlockSpec((None, None, tq, gw), lambda b, r, n: (b, r, n, c))

    def prev(c):
        return pl.BlockSpec((None, None, ATTN_RADIUS, gw),
                            lambda b, r, n: (b, r, jnp.maximum(n * hb - 1, 0), c))

    def nxt(c):
        return pl.BlockSpec((None, None, ATTN_RADIUS, gw),
                            lambda b, r, n: (b, r, jnp.minimum((n + 1) * hb, last), c))

    gain = pl.BlockSpec((1, HEAD_DIM), lambda b, r, n: (0, 0))
    out_spec = pl.BlockSpec((None, None, tq, gw), lambda b, r, n: (b, r, n, 0))
    out_shape = jax.ShapeDtypeStruct((B, dil, L, gw), F32)
    return pl.pallas_call(
        functools.partial(_attn_body, tq=tq, L=L),
        grid=(B, dil, L // tq),
        in_specs=[cur(cq), cur(ck), prev(ck), nxt(ck), cur(cv), prev(cv), nxt(cv),
                  pl.BlockSpec(bias.shape, lambda b, r, n: (0, 0, 0)), gain, gain],
        out_specs=[out_spec, pl.BlockSpec((None, None, tq, HEAD_DIM), lambda b, r, n: (b, r, n, 0))],
        out_shape=[jax.ShapeDtypeStruct(out_shape.shape, BF16),
                   jax.ShapeDtypeStruct((B, dil, L, HEAD_DIM), F32)],
        compiler_params=_params("parallel", "parallel", "arbitrary"),
        name=f"attn_g{g}",
    )(qkv, qkv, qkv, qkv, qkv, qkv, qkv, bias, q_norm, k_norm)


def _t5_bucket(rel):
    half = REL_BUCKETS // 2
    exact = half // 2
    offset = np.where(rel > 0, half, 0)
    n = np.abs(rel)
    large = exact + (np.log(np.maximum(n, 1) / exact) / np.log(REL_MAX_DISTANCE / exact)
                     * (half - exact)).astype(np.int32)
    large = np.minimum(large, half - 1)
    return (offset + np.where(n < exact, n, large)).astype(np.int32)


def _attn_bias(rel_bias, g):
    dil = ATTN_PATTERNS[g][1]
    nk = ATTN_SUB + 2 * ATTN_RADIUS
    span = nk - ATTN_RADIUS - 1
    period = 2 * span + 2
    rel = np.arange(period) - span
    table = rel_bias[:, g * ATTN_HEADS_PER_GROUP:(g + 1) * ATTN_HEADS_PER_GROUP]
    v = jnp.take(table, _t5_bucket(rel * dil), axis=0).T.astype(F32)
    skew = jnp.tile(v, (1, ATTN_SUB))[:, :ATTN_SUB * (period - 1)].reshape(-1, ATTN_SUB, period - 1)
    return skew[:, :, ATTN_SUB - 1:ATTN_SUB - 1 + nk]


def _attn_merge_body(o0_ref, o1_ref, o2_ref, l0_ref, l1_ref, l2_ref, y_ref):
    o_refs = (o0_ref, o1_ref, o2_ref)
    l_refs = (l0_ref, l1_ref, l2_ref)
    for hs in range(ATTN_HEADS_PER_GROUP):
        ls = [r[:, hs * LSE_LANES:hs * LSE_LANES + 1] for r in l_refs]
        m = jnp.maximum(jnp.maximum(ls[0], ls[1]), ls[2])
        ws = [jnp.exp(l - m) for l in ls]
        tot = ws[0] + ws[1] + ws[2]
        for g in range(len(ATTN_PATTERNS)):
            o = o_refs[g][:, hs * HEAD_DIM:(hs + 1) * HEAD_DIM].astype(F32)
            c0 = g * ATTN_GROUP_WIDTH + hs * HEAD_DIM
            y_ref[:, c0:c0 + HEAD_DIM] = (o * (ws[g] / tot)).astype(y_ref.dtype)


def _attn_merge(outs, lses, *, tm):
    T = outs[0].shape[0]
    gw = ATTN_GROUP_WIDTH
    specs = [pl.BlockSpec((tm, gw), lambda i: (i, 0)) for _ in ATTN_PATTERNS]
    return pl.pallas_call(
        _attn_merge_body,
        grid=(T // tm,),
        in_specs=specs + [pl.BlockSpec((tm, HEAD_DIM), lambda i: (i, 0)) for _ in ATTN_PATTERNS],
        out_specs=pl.BlockSpec((tm, ATTN_WIDTH), lambda i: (i, 0)),
        out_shape=jax.ShapeDtypeStruct((T, ATTN_WIDTH), BF16),
        compiler_params=_params("parallel"),
        name="attn_merge",
    )(*outs, *lses)


def _attention(proj, rel_bias, q_norm, k_norm, *, B, S, tq, tm):
    T = proj.shape[0]
    gw = ATTN_GROUP_WIDTH
    outs, lses = [], []
    for g, (_, dil) in enumerate(ATTN_PATTERNS):
        L = S // dil
        start = COL_ATT + g * 3 * gw
        if dil == 1:
            qkv, cols = proj.reshape(B, 1, S, IN_WIDTH), [start // gw + c for c in range(3)]
        else:
            qkv = proj[:, start:start + 3 * gw].reshape(B, L, dil, 3 * gw).transpose(0, 2, 1, 3)
            cols = [0, 1, 2]
        o, l = _attn_group(qkv, cols, _attn_bias(rel_bias, g), q_norm, k_norm, g=g, tq=tq)
        outs.append(o.transpose(0, 2, 1, 3).reshape(T, gw))
        lses.append(l.transpose(0, 2, 1, 3).reshape(T, HEAD_DIM))
    return _attn_merge(outs, lses, tm=tm)


def _merge_body(h_ref, g_ref, yp_ref, ya_ref, yr_ref, yc_ref,
                wg0_ref, wg1_ref, wg2_ref, wg3_ref, bg0_ref, bg1_ref, bg2_ref, bg3_ref,
                wp_ref, wa_ref, wr_ref, wc_ref, wo_ref, o_ref, u_ref):
    j = pl.program_id(1)

    @pl.when(j == 0)
    def _():
        u_ref[...] = _rms(h_ref[...], g_ref[...]).astype(BF16)
        o_ref[...] = h_ref[...]

    u = u_ref[...]
    merged = None
    for y_ref, wb_ref, wg_ref, bg_ref in ((yp_ref, wp_ref, wg0_ref, bg0_ref),
                                          (ya_ref, wa_ref, wg1_ref, bg1_ref),
                                          (yr_ref, wr_ref, wg2_ref, bg2_ref),
                                          (yc_ref, wc_ref, wg3_ref, bg3_ref)):
        gate = jax.nn.sigmoid(jnp.dot(u, wg_ref[...], preferred_element_type=F32) + bg_ref[...])
        term = gate * jnp.dot(y_ref[...], wb_ref[...], preferred_element_type=F32)
        merged = term if merged is None else merged + term
    o_ref[...] += jnp.dot(merged.astype(BF16), wo_ref[...], preferred_element_type=F32)


def _merge(h, gain, ys, w_gate, b_gate, w_brs, w_out, *, tm, tn):
    T, D = h.shape
    nb = D // tn
    row = lambda width: pl.BlockSpec((tm, width), lambda i, j: (i, 0))
    row1 = lambda width: pl.BlockSpec((tm, width), lambda i, j: (i, 0), pipeline_mode=pl.Buffered(1))
    gate_specs = [pl.BlockSpec((D, tn), functools.partial(lambda i, j, b: (0, b * nb + j), b=b))
                  for b in range(N_BRANCHES)]
    bias_specs = [pl.BlockSpec((1, tn), functools.partial(lambda i, j, b: (0, b * nb + j), b=b))
                  for b in range(N_BRANCHES)]
    br_specs = [pl.BlockSpec((w.shape[0], tn), lambda i, j: (0, j)) for w in w_brs]
    return pl.pallas_call(
        _merge_body,
        grid=(T // tm, nb),
        in_specs=[row(D), pl.BlockSpec((1, D), lambda i, j: (0, 0))]
        + [row1(y.shape[1]) for y in ys] + gate_specs + bias_specs + br_specs
        + [pl.BlockSpec((tn, D), lambda i, j: (j, 0))],
        out_specs=row(D),
        out_shape=jax.ShapeDtypeStruct((T, D), F32),
        scratch_shapes=[pltpu.VMEM((tm, D), BF16)],
        compiler_params=_params("parallel", "arbitrary"),
        name="merge",
    )(h, gain, *ys, *([w_gate] * N_BRANCHES), *([b_gate] * N_BRANCHES), *w_brs, w_out)


def _ple_body(h_ref, g_ref, p_ref, wg_ref, wp_ref, o_ref):
    h = h_ref[...]
    gate = jax.nn.sigmoid(jnp.dot(_rms(h, g_ref[...]).astype(BF16), wg_ref[...],
                                  preferred_element_type=F32))
    ple = jnp.dot(p_ref[...].astype(BF16), wp_ref[...], preferred_element_type=F32)
    o_ref[...] = h + gate * ple


def _ple(h, gain, p, w_gate, w_proj, *, tm):
    T, D = h.shape
    P = p.shape[1]
    return pl.pallas_call(
        _ple_body,
        grid=(T // tm,),
        in_specs=[
            pl.BlockSpec((tm, D), lambda i: (i, 0)),
            pl.BlockSpec((1, D), lambda i: (0, 0)),
            pl.BlockSpec((tm, P), lambda i: (i, 0)),
            pl.BlockSpec((D, D), lambda i: (0, 0)),
            pl.BlockSpec((P, D), lambda i: (0, 0)),
        ],
        out_specs=pl.BlockSpec((tm, D), lambda i: (i, 0)),
        out_shape=jax.ShapeDtypeStruct((T, D), F32),
        compiler_params=_params("parallel"),
        name="ple",
    )(h, gain, p, w_gate, w_proj)


def _permute_w_in(w_in):
    xp, aq, ak, av, rq, rk, rv, rg, cin = jnp.split(
        w_in, np.cumsum((1024, 1536, 1536, 1536, 512, 512, 1024, 1024))[:], axis=-1)
    gw = ATTN_GROUP_WIDTH
    att = [w[:, g * gw:(g + 1) * gw] for g in range(len(ATTN_PATTERNS)) for w in (aq, ak, av)]
    return jnp.concatenate([cin, xp, rv, rg] + att + [rq, rk], axis=-1)


def _rope_tables(S):
    half = RET_QK_DIM // 2
    pos = jnp.arange(S, dtype=F32)
    inv = ROPE_BASE ** (-jnp.linspace(0.0, 1.0, half, dtype=F32))
    ang = pos[:, None] * inv[None, :]
    cos, sin = jnp.cos(ang), jnp.sin(ang)
    return jnp.concatenate([cos, cos], axis=-1), jnp.concatenate([-sin, sin], axis=-1)


def kernel(x, p, rel_bias, ffn1_norm, ffn1_w_gate, ffn1_w_up, ffn1_w_down, mix_norm, w_in,
           pool_w, pool_scale, q_norm, k_norm, ret_decay_logit, ret_norm, conv_w, conv_b,
           conv_norm_g, conv_norm_b, w_gate, b_gate, w_br_pool, w_br_attn, w_br_ret, w_br_conv,
           w_out, ffn2_norm, ffn2_w_gate, ffn2_w_up, ffn2_w_down, ple_norm, w_ple_gate, w_ple_proj):
    B, S, D = x.shape
    T = B * S
    depth = p.shape[0]
    cos2, sin2 = _rope_tables(S)
    h = x.reshape(T, D)
    row = lambda v: v.reshape(1, -1)
    bf = lambda w: w.astype(BF16)
    for i in range(depth):
        h = _ffn(h, row(ffn1_norm[i]), bf(ffn1_w_gate[i]), bf(ffn1_w_up[i]), bf(ffn1_w_down[i]),
                 tm=1024, tf=512)
        proj = _inproj(h, row(mix_norm[i]), bf(_permute_w_in(w_in[i])), tm=1024, tn=1536)
        y_pool = _pool(proj, bf(pool_w[i]), row(pool_scale[i]), S=S, tm=512)
        y_attn = _attention(proj, rel_bias, row(q_norm[i]), row(k_norm[i]), B=B, S=S, tq=512, tm=512)
        y_ret = _retention(proj, cos2, sin2, ret_decay_logit[i], row(ret_norm[i]), B=B, S=S, C=256, cpb=2)
        y_conv = _conv(proj, conv_w[i], row(conv_b[i]), row(conv_norm_g[i]), row(conv_norm_b[i]),
                       S=S, tm=256)
        h = _merge(h, row(mix_norm[i]), (y_pool, y_attn, y_ret, y_conv), bf(w_gate[i]), row(b_gate[i]),
                   (bf(w_br_pool[i]), bf(w_br_attn[i]), bf(w_br_ret[i]), bf(w_br_conv[i])),
                   bf(w_out[i]), tm=512, tn=512)
        h = _ffn(h, row(ffn2_norm[i]), bf(ffn2_w_gate[i]), bf(ffn2_w_up[i]), bf(ffn2_w_down[i]),
                 tm=1024, tf=512)
        h = _ple(h, row(ple_norm[i]), p[i].reshape(T, -1), bf(w_ple_gate[i]), bf(w_ple_proj[i]), tm=512)
    return h.reshape(B, S, D)
```
